```python
import math
import jax, jax.numpy as jnp
from jax import lax
import numpy as np

D_MODEL = 1024
BATCH = 8
SEQ = 2048
DEPTH = 1
DEC_BATCH = 128
DEC_SEQ = 8
PAST_LEN = 16384
PAGE_SIZE = 128

N_META = 16
NORM_EPS = 1e-6
CONV_A_DIM = D_MODEL
CONV_A_W = 3
SSD_INNER = 2 * D_MODEL
SSD_HEADDIM = 64
SSD_HEADS = SSD_INNER // SSD_HEADDIM
SSD_STATE = 128
SSD_GROUPS = 4
SSD_HPG = SSD_HEADS // SSD_GROUPS
SSD_CONV_W = 4
SSD_CONV_DIM = SSD_INNER + 2 * SSD_GROUPS * SSD_STATE
SSD_CHUNK = 128
IN_DIM = 3 * CONV_A_DIM + SSD_INNER + SSD_CONV_DIM + SSD_HEADS + 2 * D_MODEL
PEER_HEADS = 8
N_KEYS = 128
N_EXPERTS = N_KEYS * N_KEYS
PEER_TOPK = 16
PEER_QDIM = 256
PEER_HALF = PEER_QDIM // 2
PEER_BLOCK = 256

kernel_name = 'hybrid_shortconv_ssd_peer_step'


def rmsnorm(x, w):
    xf = x.astype(jnp.float32)
    inv = lax.rsqrt(jnp.mean(xf * xf, axis=-1, keepdims=True) + NORM_EPS)
    return (xf * inv).astype(x.dtype) * w.astype(x.dtype)


def causal_dwconv(x_padded, w):
    c = x_padded.shape[-1]
    return lax.conv_general_dilated(
        x_padded, w.astype(x_padded.dtype)[:, None, :], window_strides=(1,), padding='VALID',
        dimension_numbers=('NWC', 'WIO', 'NWC'), feature_group_count=c)


def split_in_proj(proj):
    sizes = (CONV_A_DIM, CONV_A_DIM, CONV_A_DIM, SSD_INNER, SSD_CONV_DIM, SSD_HEADS, D_MODEL, D_MODEL)
    points = np.cumsum(np.array(sizes))[:-1].tolist()
    return jnp.split(proj, points, axis=-1)


def ssd_chunked(x, dt, a, bm, cm, state0, chunk):
    b, L = x.shape[0], x.shape[1]
    nc = L // chunk
    x = x.reshape(b, nc, chunk, SSD_GROUPS, SSD_HPG, SSD_HEADDIM)
    dt = dt.reshape(b, nc, chunk, SSD_GROUPS, SSD_HPG)
    bm = bm.reshape(b, nc, chunk, SSD_GROUPS, SSD_STATE)
    cm = cm.reshape(b, nc, chunk, SSD_GROUPS, SSD_STATE)
    cum = jnp.cumsum(dt * a, axis=2)
    causal = jnp.tril(jnp.ones((chunk, chunk), dtype=bool))[:, :, None, None]
    seg = cum[:, :, :, None] - cum[:, :, None, :]
    decay = jnp.exp(jnp.where(causal, seg, -jnp.inf))
    cb = jnp.einsum('bcqgn,bckgn->bcqkg', cm, bm)
    w_qk = cb[..., None] * decay * dt[:, :, None]
    y_diag = jnp.einsum('bcqkgh,bckghp->bcqghp', w_qk, x)
    decay_end = jnp.exp(cum[:, :, -1:] - cum)
    chunk_states = jnp.einsum('bckgn,bckgh,bckghp->bcghpn', bm, decay_end * dt, x)
    chunk_decay = jnp.exp(cum[:, :, -1])

    def step(s, inp):
        d, cs = inp
        return d[..., None, None] * s + cs, s

    final, prev = lax.scan(step, state0, (jnp.moveaxis(chunk_decay, 1, 0), jnp.moveaxis(chunk_states, 1, 0)))
    prev = jnp.moveaxis(prev, 0, 1)
    y_off = jnp.einsum('bcqgn,bcghpn,bcqgh->bcqghp', cm, prev, jnp.exp(cum))
    y = (y_diag + y_off).reshape(b, L, SSD_GROUPS, SSD_HPG, SSD_HEADDIM)
    return y, final


def token_mixers(h, conv_a_prev, conv_ssd_prev, ssm_prev, segments,
                 w_in, conv_a_w, w_a_out, ssd_conv_w, ssd_conv_b, ssd_dt_bias, ssd_a_log,
                 ssd_d, ssd_norm_w, w_ssd_out, w_o):
    b, L, _ = h.shape
    dty = h.dtype
    a_b, a_c, a_x, z, xbc, dt_raw, g_a, g_b = split_in_proj(h @ w_in)
    u = a_c * a_x
    u_full = jnp.concatenate([conv_a_prev.astype(dty), u], axis=1)
    br_a = (a_b * causal_dwconv(u_full, conv_a_w)) @ w_a_out
    new_conv_a = u_full[:, u_full.shape[1] - (CONV_A_W - 1):]
    xbc_full = jnp.concatenate([conv_ssd_prev.astype(dty), xbc], axis=1)
    xbc_c = jax.nn.silu(causal_dwconv(xbc_full, ssd_conv_w) + ssd_conv_b.astype(dty))
    new_conv_ssd = xbc_full[:, xbc_full.shape[1] - (SSD_CONV_W - 1):]
    xs, bm, cm = jnp.split(xbc_c, [SSD_INNER, SSD_INNER + SSD_GROUPS * SSD_STATE], axis=-1)
    xh = xs.astype(jnp.float32).reshape(b, L, SSD_GROUPS, SSD_HPG, SSD_HEADDIM)
    bm = bm.astype(jnp.float32).reshape(b, L, SSD_GROUPS, SSD_STATE)
    cm = cm.astype(jnp.float32).reshape(b, L, SSD_GROUPS, SSD_STATE)
    dt = jax.nn.softplus(dt_raw.astype(jnp.float32) + ssd_dt_bias.astype(jnp.float32))
    dt = dt.reshape(b, L, SSD_GROUPS, SSD_HPG)
    a = -jnp.exp(ssd_a_log.astype(jnp.float32)).reshape(SSD_GROUPS, SSD_HPG)
    state = ssm_prev.astype(jnp.float32).reshape(b, SSD_GROUPS, SSD_HPG, SSD_HEADDIM, SSD_STATE)
    ys = []
    start = 0
    for length, chunk in segments:
        y_seg, state = ssd_chunked(xh[:, start:start + length], dt[:, start:start + length], a,
                                   bm[:, start:start + length], cm[:, start:start + length], state, chunk)
        ys.append(y_seg)
        start += length
    y = jnp.concatenate(ys, axis=1) + ssd_d.astype(jnp.float32).reshape(SSD_GROUPS, SSD_HPG, 1) * xh
    gated = (y.reshape(b, L, SSD_INNER) * jax.nn.silu(z.astype(jnp.float32))).reshape(b, L, SSD_GROUPS, -1)
    gated = gated * lax.rsqrt(jnp.mean(gated * gated, axis=-1, keepdims=True) + NORM_EPS)
    yn = gated.reshape(b, L, SSD_INNER).astype(dty) * ssd_norm_w.astype(dty)
    br_b = yn @ w_ssd_out
    merged = jax.nn.sigmoid(g_a) * br_a + jax.nn.sigmoid(g_b) * br_b
    new_ssm = state.reshape(b, SSD_HEADS, SSD_HEADDIM, SSD_STATE).astype(ssm_prev.dtype)
    return merged @ w_o, new_conv_a, new_conv_ssd, new_ssm


def peer(h, w_q, sub_keys, u_tab, v_tab):
    b, L, d = h.shape
    t = h.reshape(-1, d)
    n_tok = t.shape[0]
    n_blk = -(-n_tok // PEER_BLOCK)
    tp = jnp.pad(t, ((0, n_blk * PEER_BLOCK - n_tok), (0, 0))).reshape(n_blk, PEER_BLOCK, d)

    def block(tb):
        q = (tb @ w_q).reshape(PEER_BLOCK, PEER_HEADS, 2, PEER_HALF)
        s = jnp.einsum('thik,hink->thin', q, sub_keys.astype(q.dtype)).astype(jnp.float32)
        s_top, i_top = lax.top_k(s, PEER_TOPK)
        cand = (s_top[:, :, 0, :, None] + s_top[:, :, 1, None, :]).reshape(PEER_BLOCK, PEER_HEADS, PEER_TOPK * PEER_TOPK)
        best_s, best_c = lax.top_k(cand, PEER_TOPK)
        i1 = jnp.take_along_axis(i_top[:, :, 0], best_c // PEER_TOPK, axis=-1)
        i2 = jnp.take_along_axis(i_top[:, :, 1], best_c % PEER_TOPK, axis=-1)
        eid = i1 * N_KEYS + i2
        g = jax.nn.softmax(best_s, axis=-1)
        act = jax.nn.gelu(jnp.einsum('thkd,td->thk', u_tab[eid], tb).astype(jnp.float32), approximate=False)
        return jnp.einsum('thk,thkd->td', (g * act).astype(tb.dtype), v_tab[eid])

    out = lax.map(block, tp).reshape(-1, d)[:n_tok]
    return out.reshape(b, L, d)


def layer(x, conv_a_prev, conv_ssd_prev, ssm_prev, segments, lw):
    (norm_mix_w, w_in, conv_a_w, w_a_out, ssd_conv_w, ssd_conv_b, ssd_dt_bias, ssd_a_log, ssd_d,
     ssd_norm_w, w_ssd_out, w_o, norm_ffn_w, peer_w_q, peer_sub_keys, peer_u, peer_v) = lw
    mix, ca, cs, ss = token_mixers(rmsnorm(x, norm_mix_w), conv_a_prev, conv_ssd_prev, ssm_prev, segments,
                                   w_in, conv_a_w, w_a_out, ssd_conv_w, ssd_conv_b, ssd_dt_bias,
                                   ssd_a_log, ssd_d, ssd_norm_w, w_ssd_out, w_o)
    x = x + mix
    x = x + peer(rmsnorm(x, norm_ffn_w), peer_w_q, peer_sub_keys, peer_u, peer_v)
    return x, ca, cs, ss


def setup_inputs(seed: int = 0) -> dict:
    key = jax.random.key(seed)
    ks = jax.random.split(key, 32)
    f32 = jnp.float32
    nrm = lambda k, shape, scale: jax.random.normal(k, shape, f32) * scale
    dt_init = jnp.exp(jax.random.uniform(ks[10], (DEPTH, SSD_HEADS), f32, math.log(1e-3), math.log(1e-1)))
    return {
        'x_prompt': nrm(ks[0], (BATCH, SEQ, D_MODEL), 1.0),
        'x_sample': nrm(ks[1], (DEC_BATCH, DEC_SEQ, D_MODEL), 1.0),
        'state_conv_a': nrm(ks[2], (DEPTH, DEC_BATCH, CONV_A_W - 1, CONV_A_DIM), 1.0),
        'state_conv_ssd': nrm(ks[3], (DEPTH, DEC_BATCH, SSD_CONV_W - 1, SSD_CONV_DIM), 1.0),
        'state_ssm': nrm(ks[4], (DEPTH, DEC_BATCH, SSD_HEADS, SSD_HEADDIM, SSD_STATE), 0.1),
        'meta_tokens': nrm(ks[5], (N_META, D_MODEL), 1.0),
        'norm_mix_w': 1.0 + nrm(ks[6], (DEPTH, D_MODEL), 0.02),
        'w_in': nrm(ks[7], (DEPTH, D_MODEL, IN_DIM), D_MODEL ** -0.5),
        'conv_a_w': nrm(ks[8], (DEPTH, CONV_A_W, CONV_A_DIM), CONV_A_W ** -0.5),
        'w_a_out': nrm(ks[9], (DEPTH, CONV_A_DIM, D_MODEL), CONV_A_DIM ** -0.5),
        'ssd_conv_w': nrm(ks[11], (DEPTH, SSD_CONV_W, SSD_CONV_DIM), SSD_CONV_W ** -0.5),
        'ssd_conv_b': nrm(ks[12], (DEPTH, SSD_CONV_DIM), 0.01),
        'ssd_dt_bias': dt_init + jnp.log(-jnp.expm1(-dt_init)),
        'ssd_a_log': jnp.log(jax.random.uniform(ks[13], (DEPTH, SSD_HEADS), f32, 1.0, 16.0)),
        'ssd_d': 1.0 + nrm(ks[14], (DEPTH, SSD_HEADS), 0.1),
        'ssd_norm_w': 1.0 + nrm(ks[15], (DEPTH, SSD_INNER), 0.02),
        'w_ssd_out': nrm(ks[16], (DEPTH, SSD_INNER, D_MODEL), SSD_INNER ** -0.5),
        'w_o': nrm(ks[17], (DEPTH, D_MODEL, D_MODEL), D_MODEL ** -0.5),
        'norm_ffn_w': 1.0 + nrm(ks[18], (DEPTH, D_MODEL), 0.02),
        'peer_w_q': nrm(ks[19], (DEPTH, D_MODEL, PEER_HEADS * PEER_QDIM), D_MODEL ** -0.5),
        'peer_sub_keys': nrm(ks[20], (DEPTH, PEER_HEADS, 2, N_KEYS, PEER_HALF), PEER_HALF ** -0.5),
        'peer_u': nrm(ks[21], (DEPTH, N_EXPERTS, D_MODEL), D_MODEL ** -0.5),
        'peer_v': nrm(ks[22], (DEPTH, N_EXPERTS, D_MODEL), PEER_HEADS ** -0.5),
        'norm_final_w': 1.0 + nrm(ks[23], (D_MODEL,), 0.02),
    }


def reference(x_prompt, x_sample, state_conv_a, state_conv_ssd, state_ssm, meta_tokens,
              norm_mix_w, w_in, conv_a_w, w_a_out, ssd_conv_w, ssd_conv_b, ssd_dt_bias, ssd_a_log,
              ssd_d, ssd_norm_w, w_ssd_out, w_o, norm_ffn_w, peer_w_q, peer_sub_keys, peer_u, peer_v,
              norm_final_w):
    bp, s_len = x_prompt.shape[0], x_prompt.shape[1]
    bs, ds_len = x_sample.shape[0], x_sample.shape[1]
    dty = x_prompt.dtype
    meta = jnp.broadcast_to(meta_tokens.astype(dty)[None], (bp, N_META, D_MODEL))
    xp = jnp.concatenate([meta, x_prompt], axis=1)
    xs = x_sample
    seg_prompt = ((N_META, N_META), (s_len, SSD_CHUNK))
    seg_sample = ((ds_len, ds_len),)
    ca_p, cs_p, ss_p, ca_s, cs_s, ss_s = [], [], [], [], [], []
    for l in range(DEPTH):
        lw = (norm_mix_w[l], w_in[l], conv_a_w[l], w_a_out[l], ssd_conv_w[l], ssd_conv_b[l],
              ssd_dt_bias[l], ssd_a_log[l], ssd_d[l], ssd_norm_w[l], w_ssd_out[l], w_o[l],
              norm_ffn_w[l], peer_w_q[l], peer_sub_keys[l], peer_u[l], peer_v[l])
        zero_ca = jnp.zeros((bp, CONV_A_W - 1, CONV_A_DIM), dty)
        zero_cs = jnp.zeros((bp, SSD_CONV_W - 1, SSD_CONV_DIM), dty)
        zero_ss = jnp.zeros((bp, SSD_HEADS, SSD_HEADDIM, SSD_STATE), jnp.float32)
        xp, a1, a2, a3 = layer(xp, zero_ca, zero_cs, zero_ss, seg_prompt, lw)
        xs, b1, b2, b3 = layer(xs, state_conv_a[l], state_conv_ssd[l], state_ssm[l], seg_sample, lw)
        ca_p.append(a1); cs_p.append(a2); ss_p.append(a3)
        ca_s.append(b1); cs_s.append(b2); ss_s.append(b3)
    y_prompt = rmsnorm(xp[:, N_META:], norm_final_w)
    y_sample = rmsnorm(xs, norm_final_w)
    return (y_prompt, y_sample, jnp.stack(ca_p), jnp.stack(cs_p), jnp.stack(ss_p),
            jnp.stack(ca_s), jnp.stack(cs_s), jnp.stack(ss_s))
```

```python
import functools
import math

import numpy as np
import jax
import jax.numpy as jnp
from jax import lax
from jax.experimental import pallas as pl
from jax.experimental.pallas import tpu as pltpu

F32 = jnp.float32
BF16 = jnp.bfloat16

D_MODEL = 1024
N_META = 16
NORM_EPS = 1e-6
CONV_A_W = 3
SSD_INNER = 2048
SSD_HEADDIM = 64
SSD_HEADS = 32
SSD_STATE = 128
SSD_GROUPS = 4
SSD_HPG = 8
SSD_CONV_W = 4
SSD_CONV_DIM = SSD_INNER + 2 * SSD_GROUPS * SSD_STATE
PEER_HEADS = 8
N_KEYS = 128
PEER_TOPK = 16
PEER_HALF = 128

ROWS = 128
LANES = 128
SUBLANES = 8
COL_AB, COL_AC, COL_AX, COL_Z, COL_XBC = 0, 1024, 2048, 3072, 5120
COL_G = 8192
COL_DT = 10240
IN_DIM_R = COL_DT + LANES
MAIN_W = COL_G
NEG_INF = float("-inf")

VMEM_LIMIT = 56 * 1024 * 1024


def _cparams(sem):
    return pltpu.CompilerParams(dimension_semantics=sem, vmem_limit_bytes=VMEM_LIMIT)


def _inproj_kernel(x_ref, nw_ref, w_ref, o_ref, hn_ref):
    @pl.when(pl.program_id(1) == 0)
    def _():
        x = x_ref[...]
        inv = lax.rsqrt(jnp.mean(x * x, axis=-1, keepdims=True) + NORM_EPS)
        hn_ref[...] = ((x * inv) * nw_ref[...]).astype(BF16)

    o_ref[...] = jnp.dot(hn_ref[...], w_ref[...], preferred_element_type=F32)


def _inproj(x_all, norm_w, w_in_r, tt, nt):
    t = x_all.shape[0]
    n = w_in_r.shape[1]
    return pl.pallas_call(
        _inproj_kernel,
        grid=(t // tt, n // nt),
        in_specs=[
            pl.BlockSpec((tt, D_MODEL), lambda i, j: (i, 0)),
            pl.BlockSpec((1, D_MODEL), lambda i, j: (0, 0)),
            pl.BlockSpec((D_MODEL, nt), lambda i, j: (0, j)),
        ],
        out_specs=pl.BlockSpec((tt, nt), lambda i, j: (i, j)),
        out_shape=jax.ShapeDtypeStruct((t, n), F32),
        scratch_shapes=[pltpu.VMEM((tt, D_MODEL), BF16)],
        compiler_params=_cparams(("arbitrary", "arbitrary")),
        name="inproj",
    )(x_all, norm_w, w_in_r)


def _silu(v):
    return v * jax.nn.sigmoid(v)


def _mixer_block(a_b, u, z, xbc, dt_raw, r0, su_ref, sx_ref, s_ref, gated_ref,
                 caw_ref, scw_ref, scb_ref, dtb_ref, alog_ref, dexp_ref, nrm_ref):
    R = ROWS
    su_ref[8:8 + R, :] = u
    u1 = su_ref[7:7 + R, :]
    u2 = su_ref[6:6 + R, :]
    conv_a = caw_ref[2:3, :] * u + caw_ref[1:2, :] * u1 + caw_ref[0:1, :] * u2
    ya = (a_b * conv_a).astype(BF16)
    su_ref[0:8, :] = su_ref[R:R + 8, :]

    sx_ref[8:8 + R, :] = xbc
    x1 = sx_ref[7:7 + R, :]
    x2 = sx_ref[6:6 + R, :]
    x3 = sx_ref[5:5 + R, :]
    xc = (scw_ref[3:4, :] * xbc + scw_ref[2:3, :] * x1 + scw_ref[1:2, :] * x2
          + scw_ref[0:1, :] * x3 + scb_ref[...])
    xc = _silu(xc)
    sx_ref[0:8, :] = sx_ref[R:R + 8, :]

    row = lax.broadcasted_iota(jnp.int32, (R, LANES), 0)
    col = lax.broadcasted_iota(jnp.int32, (R, LANES), 1)
    valid = row >= r0
    dt = jnp.where(valid, jax.nn.softplus(dt_raw + dtb_ref[...]), 0.0)
    a = -jnp.exp(alog_ref[...])
    da = dt * a
    tri = row >= col
    tri_f = tri.astype(F32)
    cum = jnp.dot(tri_f, da, preferred_element_type=F32, precision=lax.Precision.HIGHEST)
    da_t = da.T
    dt_t = dt.T
    triu_f = (row <= col).astype(F32)
    cum_t = jnp.dot(da_t, triu_f, preferred_element_type=F32, precision=lax.Precision.HIGHEST)
    ecum = jnp.exp(cum)
    cum_last = cum[R - 1:R, :]
    wend = jnp.exp(cum_last - cum) * dt
    dec_h = jnp.broadcast_to(jnp.exp(cum_t[:, R - 1:R]), (LANES, LANES))

    lane_lo = col < SSD_HEADDIM
    for g in range(SSD_GROUPS):
        b_g = xc[:, SSD_INNER + g * SSD_STATE: SSD_INNER + (g + 1) * SSD_STATE]
        c_g = xc[:, SSD_INNER + SSD_GROUPS * SSD_STATE + g * SSD_STATE:
                 SSD_INNER + SSD_GROUPS * SSD_STATE + (g + 1) * SSD_STATE]
        b_bf = b_g.astype(BF16)
        c_bf = c_g.astype(BF16)
        cb = lax.dot_general(c_bf, b_bf, (((1,), (1,)), ((), ())), preferred_element_type=F32)
        s_g = s_ref[g * 512:(g + 1) * 512, :]
        yo_g = lax.dot_general(c_bf, s_g.astype(BF16), (((1,), (1,)), ((), ())),
                               preferred_element_type=F32)
        for pr in range(SSD_HPG // 2):
            h0 = g * SSD_HPG + 2 * pr
            h1 = h0 + 1
            c0 = (h0 * SSD_HEADDIM)
            x_pair = xc[:, c0:c0 + LANES]
            lmats = []
            for h in (h0, h1):
                seg = jnp.where(tri, cum[:, h:h + 1] - cum_t[h:h + 1, :], NEG_INF)
                lmats.append((cb * jnp.exp(seg) * dt_t[h:h + 1, :]).astype(BF16))
            x_lo = jnp.where(lane_lo, x_pair, 0.0).astype(BF16)
            x_hi = jnp.where(lane_lo, 0.0, x_pair).astype(BF16)
            yd = (jnp.dot(lmats[0], x_lo, preferred_element_type=F32)
                  + jnp.dot(lmats[1], x_hi, preferred_element_type=F32))
            scal = jnp.where(lane_lo, ecum[:, h0:h0 + 1], ecum[:, h1:h1 + 1])
            yo = yo_g[:, 2 * pr * SSD_HEADDIM: 2 * pr * SSD_HEADDIM + LANES]
            y = yd + yo * scal + dexp_ref[:, c0:c0 + LANES] * x_pair
            zz = z[:, c0:c0 + LANES]
            gated_ref[:, c0:c0 + LANES] = y * _silu(zz)
            wsel = jnp.where(lane_lo, wend[:, h0:h0 + 1], wend[:, h1:h1 + 1])
            xw_t = (x_pair * wsel).T.astype(BF16)
            cs = jnp.dot(xw_t, b_bf, preferred_element_type=F32)
            dec = jnp.concatenate(
                [jnp.broadcast_to(dec_h[h0:h0 + 1, :], (SSD_HEADDIM, LANES)),
                 jnp.broadcast_to(dec_h[h1:h1 + 1, :], (SSD_HEADDIM, LANES))], axis=0)
            r_lo = h0 * SSD_HEADDIM
            s_ref[r_lo:r_lo + LANES, :] = dec * s_ref[r_lo:r_lo + LANES, :] + cs

    outs = []
    for g in range(SSD_GROUPS):
        gg = gated_ref[:, g * 512:(g + 1) * 512]
        ms = jnp.mean(gg * gg, axis=-1, keepdims=True)
        outs.append(((gg * lax.rsqrt(ms + NORM_EPS)).astype(F32)
                     * nrm_ref[:, g * 512:(g + 1) * 512]).astype(BF16))
    yn = jnp.concatenate(outs, axis=1)
    return ya, yn


def _mixer_prompt_kernel(main_ref, dt_ref, caw_ref, scw_ref, scb_ref, dtb_ref, alog_ref,
                         dexp_ref, nrm_ref,
                         ya_ref, yn_ref, nca_ref, ncs_ref, nss_ref,
                         su_ref, sx_ref, s_ref, gated_ref):
    j = pl.program_id(1)

    @pl.when(j == 0)
    def _():
        su_ref[0:8, :] = jnp.zeros((8, D_MODEL), F32)
        sx_ref[0:8, :] = jnp.zeros((8, SSD_CONV_DIM), F32)
        s_ref[...] = jnp.zeros_like(s_ref)

    a_b = main_ref[:, COL_AB:COL_AB + 1024]
    u = main_ref[:, COL_AC:COL_AC + 1024] * main_ref[:, COL_AX:COL_AX + 1024]
    z = main_ref[:, COL_Z:COL_Z + 2048]
    xbc = main_ref[:, COL_XBC:COL_XBC + SSD_CONV_DIM]
    r0 = jnp.where(j == 0, ROWS - N_META, 0)
    ya, yn = _mixer_block(a_b, u, z, xbc, dt_ref[...], r0, su_ref, sx_ref, s_ref, gated_ref,
                          caw_ref, scw_ref, scb_ref, dtb_ref, alog_ref, dexp_ref, nrm_ref)
    ya_ref[...] = ya
    yn_ref[...] = yn
    nca_ref[0] = su_ref[6:8, :]
    ncs_ref[0] = sx_ref[5:8, :]
    nss_ref[0] = s_ref[...]


def _mixer_sample_kernel(main_ref, dt_ref, ca0_ref, cs0_ref, ss0_ref,
                         caw_ref, scw_ref, scb_ref, dtb_ref, alog_ref, dexp_ref, nrm_ref,
                         ya_in, yn_in,
                         ya_ref, yn_ref, nca_ref, ncs_ref, nss_ref,
                         su_ref, sx_ref, s_ref, gated_ref, pad_ref, tmp_ref):
    del ya_in, yn_in
    R = ROWS
    n = 8
    su_ref[0:8, :] = jnp.zeros((8, D_MODEL), F32)
    sx_ref[0:8, :] = jnp.zeros((8, SSD_CONV_DIM), F32)
    s_ref[...] = ss0_ref[0]

    pad_ref[...] = jnp.zeros_like(pad_ref)
    pad_ref[R - n:R, :] = main_ref[...]
    a_b = pad_ref[:, COL_AB:COL_AB + 1024]
    z = pad_ref[:, COL_Z:COL_Z + 2048]
    tmp_ref[...] = jnp.zeros_like(tmp_ref)
    tmp_ref[6:8, 0:D_MODEL] = ca0_ref[0]
    u = jnp.concatenate(
        [jnp.zeros((R - 2 * n, D_MODEL), F32), tmp_ref[:, 0:D_MODEL],
         main_ref[:, COL_AC:COL_AC + 1024] * main_ref[:, COL_AX:COL_AX + 1024]], axis=0)
    tmp_ref[...] = jnp.zeros_like(tmp_ref)
    tmp_ref[5:8, :] = cs0_ref[0]
    xbc = jnp.concatenate(
        [jnp.zeros((R - 2 * n, SSD_CONV_DIM), F32), tmp_ref[...],
         main_ref[:, COL_XBC:COL_XBC + SSD_CONV_DIM]], axis=0)
    dt_raw = jnp.concatenate([jnp.zeros((R - n, LANES), F32), dt_ref[...]], axis=0)
    ya, yn = _mixer_block(a_b, u, z, xbc, dt_raw, R - n, su_ref, sx_ref, s_ref, gated_ref,
                          caw_ref, scw_ref, scb_ref, dtb_ref, alog_ref, dexp_ref, nrm_ref)
    ya_ref[...] = ya[R - n:R, :]
    yn_ref[...] = yn[R - n:R, :]
    nca_ref[0] = su_ref[6:8, :]
    ncs_ref[0] = sx_ref[5:8, :]
    nss_ref[0] = s_ref[...]


def _mixer_weight_specs(nidx):
    zero = (lambda *a: (0, 0))
    del nidx
    return [
        pl.BlockSpec((CONV_A_W, D_MODEL), zero),
        pl.BlockSpec((SSD_CONV_W, SSD_CONV_DIM), zero),
        pl.BlockSpec((1, SSD_CONV_DIM), zero),
        pl.BlockSpec((1, LANES), zero),
        pl.BlockSpec((1, LANES), zero),
        pl.BlockSpec((1, SSD_INNER), zero),
        pl.BlockSpec((1, SSD_INNER), zero),
    ]


def _mixer_scratch():
    return [
        pltpu.VMEM((ROWS + 8, D_MODEL), F32),
        pltpu.VMEM((ROWS + 8, SSD_CONV_DIM), F32),
        pltpu.VMEM((SSD_INNER, SSD_STATE), F32),
        pltpu.VMEM((ROWS, SSD_INNER), F32),
    ]


def _mixer_prompt(proj, weights, nseq, nblk, t_all):
    dt_blk = COL_DT // LANES
    return pl.pallas_call(
        _mixer_prompt_kernel,
        grid=(nseq, nblk),
        in_specs=[
            pl.BlockSpec((ROWS, MAIN_W), lambda b, j: (b * nblk + j, 0)),
            pl.BlockSpec((ROWS, LANES), lambda b, j: (b * nblk + j, dt_blk)),
        ] + _mixer_weight_specs(2),
        out_specs=[
            pl.BlockSpec((ROWS, D_MODEL), lambda b, j: (b * nblk + j, 0)),
            pl.BlockSpec((ROWS, SSD_INNER), lambda b, j: (b * nblk + j, 0)),
            pl.BlockSpec((1, CONV_A_W - 1, D_MODEL), lambda b, j: (b, 0, 0)),
            pl.BlockSpec((1, SSD_CONV_W - 1, SSD_CONV_DIM), lambda b, j: (b, 0, 0)),
            pl.BlockSpec((1, SSD_INNER, SSD_STATE), lambda b, j: (b, 0, 0)),
        ],
        out_shape=[
            jax.ShapeDtypeStruct((t_all, D_MODEL), BF16),
            jax.ShapeDtypeStruct((t_all, SSD_INNER), BF16),
            jax.ShapeDtypeStruct((nseq, CONV_A_W - 1, D_MODEL), F32),
            jax.ShapeDtypeStruct((nseq, SSD_CONV_W - 1, SSD_CONV_DIM), F32),
            jax.ShapeDtypeStruct((nseq, SSD_INNER, SSD_STATE), F32),
        ],
        scratch_shapes=_mixer_scratch(),
        compiler_params=_cparams(("arbitrary", "arbitrary")),
        name="mixer_prompt",
    )(proj, proj, *weights)


def _mixer_sample(proj, ca0, cs0, ss0, weights, ya_all, yn_all, nseq, row0):
    dt_blk = COL_DT // LANES
    blk0 = row0 // 8
    return pl.pallas_call(
        _mixer_sample_kernel,
        grid=(nseq,),
        in_specs=[
            pl.BlockSpec((8, MAIN_W), lambda s: (blk0 + s, 0)),
            pl.BlockSpec((8, LANES), lambda s: (blk0 + s, dt_blk)),
            pl.BlockSpec((1, CONV_A_W - 1, D_MODEL), lambda s: (s, 0, 0)),
            pl.BlockSpec((1, SSD_CONV_W - 1, SSD_CONV_DIM), lambda s: (s, 0, 0)),
            pl.BlockSpec((1, SSD_INNER, SSD_STATE), lambda s: (s, 0, 0)),
        ] + _mixer_weight_specs(1) + [
            pl.BlockSpec(memory_space=pl.ANY),
            pl.BlockSpec(memory_space=pl.ANY),
        ],
        out_specs=[
            pl.BlockSpec((8, D_MODEL), lambda s: (blk0 + s, 0)),
            pl.BlockSpec((8, SSD_INNER), lambda s: (blk0 + s, 0)),
            pl.BlockSpec((1, CONV_A_W - 1, D_MODEL), lambda s: (s, 0, 0)),
            pl.BlockSpec((1, SSD_CONV_W - 1, SSD_CONV_DIM), lambda s: (s, 0, 0)),
            pl.BlockSpec((1, SSD_INNER, SSD_STATE), lambda s: (s, 0, 0)),
        ],
        out_shape=[
            jax.ShapeDtypeStruct(ya_all.shape, BF16),
            jax.ShapeDtypeStruct(yn_all.shape, BF16),
            jax.ShapeDtypeStruct((nseq, CONV_A_W - 1, D_MODEL), F32),
            jax.ShapeDtypeStruct((nseq, SSD_CONV_W - 1, SSD_CONV_DIM), F32),
            jax.ShapeDtypeStruct((nseq, SSD_INNER, SSD_STATE), F32),
        ],
        scratch_shapes=_mixer_scratch() + [
            pltpu.VMEM((ROWS, MAIN_W), F32),
            pltpu.VMEM((8, SSD_CONV_DIM), F32),
        ],
        input_output_aliases={12: 0, 13: 1},
        compiler_params=_cparams(("arbitrary",)),
        name="mixer_sample",
    )(proj, proj, ca0, cs0, ss0, *weights, ya_all, yn_all)


def _merge_kernel(x_ref, ya_ref, yn_ref, g_ref, wa_ref, ws_ref, wo_ref, nw_ref, x2_ref, hn_ref):
    br_a = jnp.dot(ya_ref[...], wa_ref[...], preferred_element_type=F32)
    br_b = jnp.dot(yn_ref[...], ws_ref[...], preferred_element_type=F32)
    g = g_ref[...]
    merged = jax.nn.sigmoid(g[:, :D_MODEL]) * br_a + jax.nn.sigmoid(g[:, D_MODEL:]) * br_b
    mix = jnp.dot(merged.astype(BF16), wo_ref[...], preferred_element_type=F32)
    x2 = x_ref[...] + mix
    x2_ref[...] = x2
    inv = lax.rsqrt(jnp.mean(x2 * x2, axis=-1, keepdims=True) + NORM_EPS)
    hn_ref[...] = ((x2 * inv) * nw_ref[...]).astype(BF16)


def _merge(x_all, ya, yn, proj, wa, ws, wo, nw, tt):
    t = x_all.shape[0]
    gblk = COL_G // (2 * D_MODEL)
    zero = lambda i: (0, 0)
    return pl.pallas_call(
        _merge_kernel,
        grid=(t // tt,),
        in_specs=[
            pl.BlockSpec((tt, D_MODEL), lambda i: (i, 0)),
            pl.BlockSpec((tt, D_MODEL), lambda i: (i, 0)),
            pl.BlockSpec((tt, SSD_INNER), lambda i: (i, 0)),
            pl.BlockSpec((tt, 2 * D_MODEL), lambda i: (i, gblk)),
            pl.BlockSpec((D_MODEL, D_MODEL), zero),
            pl.BlockSpec((SSD_INNER, D_MODEL), zero),
            pl.BlockSpec((D_MODEL, D_MODEL), zero),
            pl.BlockSpec((1, D_MODEL), zero),
        ],
        out_specs=[
            pl.BlockSpec((tt, D_MODEL), lambda i: (i, 0)),
            pl.BlockSpec((tt, D_MODEL), lambda i: (i, 0)),
        ],
        out_shape=[
            jax.ShapeDtypeStruct((t, D_MODEL), F32),
            jax.ShapeDtypeStruct((t, D_MODEL), BF16),
        ],
        compiler_params=_cparams(("arbitrary",)),
        name="merge",
    )(x_all, ya, yn, proj, wa, ws, wo, nw)


def _oddeven_merge_sort_pairs(n):
    pairs = []
    p = 1
    while p < n:
        k = p
        while k >= 1:
            j = k % p
            while j <= n - 1 - k:
                for i in range(min(k, n - j - k)):
                    if (i + j) // (2 * p) == (i + j + k) // (2 * p):
                        pairs.append((i + j, i + j + k))
                j += 2 * k
            k //= 2
        p *= 2
    return pairs


_SORT16 = _oddeven_merge_sort_pairs(16)
_BITONIC16 = [(i, i + d) for d in (8, 4, 2, 1) for i in range(16) if (i & d) == 0]
_CAND_PAIRS = [(r1, r2) for r1 in range(PEER_TOPK) for r2 in range(PEER_TOPK)
               if (r1 + 1) * (r2 + 1) <= PEER_TOPK]


def _ce(v, i, j):
    hi = jnp.maximum(v[i], v[j])
    lo = jnp.minimum(v[i], v[j])
    v[i], v[j] = hi, lo


def _top16_desc(v):
    v = list(v)
    for i, j in _SORT16:
        _ce(v, i, j)
    for shift in (4, 2, 1):
        p = [pltpu.roll(x, shift, 0) for x in v]
        v = [jnp.maximum(v[r], p[15 - r]) for r in range(16)]
        for i, j in _BITONIC16:
            _ce(v, i, j)
    return v


def _route_kernel(ht_ref, wq_ref, keys_ref, s_ref, e_ref, tau_ref, zinv_ref, q_ref):
    tt = ht_ref.shape[1]
    q_ref[...] = jnp.dot(wq_ref[...], ht_ref[...], preferred_element_type=F32).astype(BF16)
    sub = lax.broadcasted_iota(jnp.int32, (SUBLANES, tt), 0)
    tops = [[None] * PEER_TOPK, [None] * PEER_TOPK]
    for h in range(PEER_HEADS):
        for i in range(2):
            r0 = (h * 2 + i) * PEER_HALF
            s = jnp.dot(keys_ref[h * 2 + i], q_ref[r0:r0 + PEER_HALF, :],
                        preferred_element_type=F32)
            v = [s[8 * j:8 * j + 8, :] for j in range(16)]
            top = _top16_desc(v)
            thr, best = top[PEER_TOPK - 1], top[0]
            for j in range(16):
                s_ref[i, h, 8 * j:8 * j + 8, :] = jnp.where(v[j] >= thr, v[j], NEG_INF)
                e_ref[i, h, 8 * j:8 * j + 8, :] = jnp.exp(v[j] - best)
            for r in range(PEER_TOPK):
                tops[i][r] = top[r] if h == 0 else jnp.where(sub == h, top[r], tops[i][r])
    cand = [tops[0][r1] + tops[1][r2] for r1, r2 in _CAND_PAIRS]
    picked = []
    for _ in range(PEER_TOPK):
        m = cand[0]
        for c in cand[1:]:
            m = jnp.maximum(m, c)
        picked.append(m)
        found = jnp.zeros(m.shape, jnp.bool_)
        nxt = []
        for c in cand:
            eq = c == m
            take = jnp.logical_and(eq, jnp.logical_not(found))
            found = jnp.logical_or(found, eq)
            nxt.append(jnp.where(take, NEG_INF, c))
        cand = nxt
    zsum = jnp.zeros_like(picked[0])
    for c in picked:
        zsum = zsum + jnp.exp(c - picked[0])
    tau_ref[...] = picked[PEER_TOPK - 1]
    zinv_ref[...] = 1.0 / zsum


def _route(hn_t, wq_t, keys, tt):
    t = hn_t.shape[1]
    return pl.pallas_call(
        _route_kernel,
        grid=(t // tt,),
        in_specs=[
            pl.BlockSpec((D_MODEL, tt), lambda i: (0, i)),
            pl.BlockSpec(wq_t.shape, lambda i: (0, 0)),
            pl.BlockSpec(keys.shape, lambda i: (0, 0, 0)),
        ],
        out_specs=[
            pl.BlockSpec((2, PEER_HEADS, N_KEYS, tt), lambda i: (0, 0, 0, i)),
            pl.BlockSpec((2, PEER_HEADS, N_KEYS, tt), lambda i: (0, 0, 0, i)),
            pl.BlockSpec((PEER_HEADS, tt), lambda i: (0, i)),
            pl.BlockSpec((PEER_HEADS, tt), lambda i: (0, i)),
        ],
        out_shape=[
            jax.ShapeDtypeStruct((2, PEER_HEADS, N_KEYS, t), F32),
            jax.ShapeDtypeStruct((2, PEER_HEADS, N_KEYS, t), F32),
            jax.ShapeDtypeStruct((PEER_HEADS, t), F32),
            jax.ShapeDtypeStruct((PEER_HEADS, t), F32),
        ],
        scratch_shapes=[pltpu.VMEM((wq_t.shape[0], tt), BF16)],
        compiler_params=_cparams(("arbitrary",)),
        name="peer_route",
    )(hn_t, wq_t, keys)


I1_PER_STEP = 8
EXPERTS_PER_STEP = I1_PER_STEP * N_KEYS


def _expert_kernel(ht_ref, u_ref, vt_ref, s1_ref, e1_ref, s2_ref, e2_ref, tau_ref, zinv_ref,
                   x2_ref, nfw_ref, y_ref, acc_ref, w_ref):
    c = pl.program_id(1)
    tt = ht_ref.shape[1]

    @pl.when(c == 0)
    def _():
        acc_ref[...] = jnp.zeros_like(acc_ref)

    act = jnp.dot(u_ref[...], ht_ref[...], preferred_element_type=F32)
    w_ref[...] = 0.5 * act * (1.0 + lax.erf(act * math.sqrt(0.5)))

    for half in range(tt // LANES):
        ls = pl.ds(half * LANES, LANES)
        taub = [jnp.broadcast_to(tau_ref[h:h + 1, ls], (SUBLANES, LANES))
                for h in range(PEER_HEADS)]
        for j in range(I1_PER_STEP):
            s1b = [jnp.broadcast_to(s1_ref[h, 0, j:j + 1, ls], (SUBLANES, LANES))
                   for h in range(PEER_HEADS)]
            e1b = [jnp.broadcast_to(e1_ref[h, 0, j:j + 1, ls] * zinv_ref[h:h + 1, ls],
                                    (SUBLANES, LANES)) for h in range(PEER_HEADS)]

            def body(blk, carry):
                r = pl.multiple_of(blk * SUBLANES, SUBLANES)
                g = jnp.zeros((SUBLANES, LANES), F32)
                for h in range(PEER_HEADS):
                    v = s1b[h] + s2_ref[h, pl.ds(r, SUBLANES), ls]
                    g = g + jnp.where(v >= taub[h], e1b[h] * e2_ref[h, pl.ds(r, SUBLANES), ls], 0.0)
                rr = pl.multiple_of(j * N_KEYS + blk * SUBLANES, SUBLANES)
                w_ref[pl.ds(rr, SUBLANES), ls] = w_ref[pl.ds(rr, SUBLANES), ls] * g
                return carry

            lax.fori_loop(0, N_KEYS // SUBLANES, body, 0)

    acc_ref[...] += jnp.dot(vt_ref[...], w_ref[...].astype(BF16), preferred_element_type=F32)

    @pl.when(c == pl.num_programs(1) - 1)
    def _():
        x3 = x2_ref[...] + acc_ref[...].T
        inv = lax.rsqrt(jnp.mean(x3 * x3, axis=-1, keepdims=True) + NORM_EPS)
        y_ref[...] = (x3 * inv) * nfw_ref[...]


def _experts(hn_t, u_bf, v_t, s_m, e_x, tau, zinv, x2, nfw, tt):
    t = hn_t.shape[1]
    n_exp = u_bf.shape[0]
    nc = n_exp // EXPERTS_PER_STEP
    s1 = s_m[0].reshape(PEER_HEADS, nc, I1_PER_STEP, t)
    e1 = e_x[0].reshape(PEER_HEADS, nc, I1_PER_STEP, t)
    s2 = s_m[1]
    e2 = e_x[1]
    return pl.pallas_call(
        _expert_kernel,
        grid=(t // tt, nc),
        in_specs=[
            pl.BlockSpec((D_MODEL, tt), lambda i, c: (0, i)),
            pl.BlockSpec((EXPERTS_PER_STEP, D_MODEL), lambda i, c: (c, 0)),
            pl.BlockSpec((D_MODEL, EXPERTS_PER_STEP), lambda i, c: (0, c)),
            pl.BlockSpec((PEER_HEADS, 1, I1_PER_STEP, tt), lambda i, c: (0, c, 0, i)),
            pl.BlockSpec((PEER_HEADS, 1, I1_PER_STEP, tt), lambda i, c: (0, c, 0, i)),
            pl.BlockSpec((PEER_HEADS, N_KEYS, tt), lambda i, c: (0, 0, i)),
            pl.BlockSpec((PEER_HEADS, N_KEYS, tt), lambda i, c: (0, 0, i)),
            pl.BlockSpec((PEER_HEADS, tt), lambda i, c: (0, i)),
            pl.BlockSpec((PEER_HEADS, tt), lambda i, c: (0, i)),
            pl.BlockSpec((tt, D_MODEL), lambda i, c: (i, 0)),
            pl.BlockSpec((1, D_MODEL), lambda i, c: (0, 0)),
        ],
        out_specs=pl.BlockSpec((tt, D_MODEL), lambda i, c: (i, 0)),
        out_shape=jax.ShapeDtypeStruct((t, D_MODEL), F32),
        scratch_shapes=[
            pltpu.VMEM((D_MODEL, tt), F32),
            pltpu.VMEM((EXPERTS_PER_STEP, tt), F32),
        ],
        compiler_params=_cparams(("arbitrary", "arbitrary")),
        name="peer_experts",
    )(hn_t, u_bf, v_t, s1, e1, s2, e2, tau, zinv, x2, nfw)


def _pick_tile(t, prefs):
    for p in prefs:
        if t % p == 0:
            return p
    raise ValueError(f"token count {t} not tileable by {prefs}")


def _layer(x_all, nseq_p, nblk, ca0, cs0, ss0, lw, norm_final_w):
    (norm_mix_w, w_in, conv_a_w, w_a_out, ssd_conv_w, ssd_conv_b, ssd_dt_bias, ssd_a_log, ssd_d,
     ssd_norm_w, w_ssd_out, w_o, norm_ffn_w, peer_w_q, peer_sub_keys, peer_u, peer_v) = lw
    t_all = x_all.shape[0]
    t_p = nseq_p * nblk * ROWS
    nseq_s = ca0.shape[0]

    c = np.cumsum([0, 1024, 1024, 1024, SSD_INNER, SSD_CONV_DIM, SSD_HEADS, 1024, 1024])
    w_in_r = jnp.concatenate(
        [w_in[:, c[0]:c[5]], w_in[:, c[6]:c[8]], w_in[:, c[5]:c[6]],
         jnp.zeros((D_MODEL, LANES - SSD_HEADS), w_in.dtype)], axis=1).astype(BF16)
    pad_h = lambda v: jnp.pad(v.astype(F32), (0, LANES - SSD_HEADS)).reshape(1, LANES)
    mixer_w = (conv_a_w.astype(F32), ssd_conv_w.astype(F32), ssd_conv_b.reshape(1, -1).astype(F32),
               pad_h(ssd_dt_bias), pad_h(ssd_a_log),
               jnp.repeat(ssd_d.astype(F32), SSD_HEADDIM).reshape(1, SSD_INNER),
               ssd_norm_w.reshape(1, SSD_INNER).astype(F32))

    proj = _inproj(x_all, norm_mix_w.reshape(1, D_MODEL), w_in_r,
                   _pick_tile(t_all, (1024, 512, 256)), 1152)
    ya, yn, nca_p, ncs_p, nss_p = _mixer_prompt(proj, mixer_w, nseq_p, nblk, t_all)
    ya, yn, nca_s, ncs_s, nss_s = _mixer_sample(
        proj, ca0, cs0, ss0.reshape(nseq_s, SSD_INNER, SSD_STATE), mixer_w, ya, yn, nseq_s, t_p)

    x2, hn2 = _merge(x_all, ya, yn, proj, w_a_out.astype(BF16), w_ssd_out.astype(BF16),
                     w_o.astype(BF16), norm_ffn_w.reshape(1, D_MODEL),
                     _pick_tile(t_all, (512, 256)))

    hn_t = hn2.T
    wq_t = peer_w_q.T.astype(BF16)
    keys = peer_sub_keys.reshape(PEER_HEADS * 2, N_KEYS, PEER_HALF).astype(BF16)
    tt = _pick_tile(t_all, (256, 128))
    s_m, e_x, tau, zinv = _route(hn_t, wq_t, keys, tt)
    y = _experts(hn_t, peer_u.astype(BF16), peer_v.T.astype(BF16), s_m, e_x, tau, zinv, x2,
                 norm_final_w.reshape(1, D_MODEL), tt)
    shp = (SSD_HEADS, SSD_HEADDIM, SSD_STATE)
    return (y, nca_p, ncs_p, nss_p.reshape(nseq_p, *shp), nca_s, ncs_s, nss_s.reshape(nseq_s, *shp))


def kernel(x_prompt, x_sample, state_conv_a, state_conv_ssd, state_ssm, meta_tokens, norm_mix_w, w_in, conv_a_w, w_a_out, ssd_conv_w, ssd_conv_b, ssd_dt_bias, ssd_a_log, ssd_d, ssd_norm_w, w_ssd_out, w_o, norm_ffn_w, peer_w_q, peer_sub_keys, peer_u, peer_v, norm_final_w):
    bp, s_len, d = x_prompt.shape
    bs, ds_len, _ = x_sample.shape
    depth = w_in.shape[0]
    assert depth == 1 and d == D_MODEL and ds_len == 8 and s_len % ROWS == 0
    nblk = s_len // ROWS + 1
    head = jnp.concatenate(
        [jnp.zeros((ROWS - N_META, d), x_prompt.dtype), meta_tokens.astype(x_prompt.dtype)], axis=0)
    xp = jnp.concatenate([jnp.broadcast_to(head[None], (bp, ROWS, d)), x_prompt], axis=1)
    x_all = jnp.concatenate([xp.reshape(-1, d), x_sample.reshape(-1, d)], axis=0)
    lw = (norm_mix_w[0], w_in[0], conv_a_w[0], w_a_out[0], ssd_conv_w[0], ssd_conv_b[0],
          ssd_dt_bias[0], ssd_a_log[0], ssd_d[0], ssd_norm_w[0], w_ssd_out[0], w_o[0],
          norm_ffn_w[0], peer_w_q[0], peer_sub_keys[0], peer_u[0], peer_v[0])
    y, nca_p, ncs_p, nss_p, nca_s, ncs_s, nss_s = _layer(
        x_all, bp, nblk, state_conv_a[0], state_conv_ssd[0], state_ssm[0], lw, norm_final_w)
    t_p = bp * nblk * ROWS
    y_prompt = y[:t_p].reshape(bp, nblk * ROWS, d)[:, ROWS:]
    y_sample = y[t_p:].reshape(bs, ds_len, d)
    return (y_prompt, y_sample, nca_p[None], ncs_p[None], nss_p[None],
            nca_s[None], ncs_s[None], nss_s[None])
```

```python
import functools
import math

import numpy as np
import jax
import jax.numpy as jnp
from jax import lax
from jax.experimental import pallas as pl
from jax.experimental.pallas import tpu as pltpu

F32 = jnp.float32
BF16 = jnp.bfloat16

D_MODEL = 1024
N_META = 16
NORM_EPS = 1e-6
CONV_A_W = 3
SSD_INNER = 2048
SSD_HEADDIM = 64
SSD_HEADS = 32
SSD_STATE = 128
SSD_GROUPS = 4
SSD_HPG = 8
SSD_CONV_W = 4
SSD_CONV_DIM = SSD_INNER + 2 * SSD_GROUPS * SSD_STATE
PEER_HEADS = 8
N_KEYS = 128
PEER_TOPK = 16
PEER_HALF = 128

ROWS = 128
LANES = 128
SUBLANES = 8
BF16_ROWS = 16
COL_AB, COL_AC, COL_AX, COL_Z, COL_XBC = 0, 1024, 2048, 3072, 5120
COL_G = 8192
COL_DT = 10240
IN_DIM_R = COL_DT + LANES
MAIN_W = COL_G
NEG_INF = float("-inf")

VMEM_LIMIT = 56 * 1024 * 1024


def _cparams(sem):
    return pltpu.CompilerParams(dimension_semantics=sem, vmem_limit_bytes=VMEM_LIMIT)


def _inproj_kernel(x_ref, nw_ref, w_ref, o_ref, hn_ref):
    @pl.when(pl.program_id(1) == 0)
    def _():
        x = x_ref[...]
        inv = lax.rsqrt(jnp.mean(x * x, axis=-1, keepdims=True) + NORM_EPS)
        hn_ref[...] = ((x * inv) * nw_ref[...]).astype(BF16)

    o_ref[...] = jnp.dot(hn_ref[...], w_ref[...], preferred_element_type=F32)


def _inproj(x_all, norm_w, w_in_r, tt, nt):
    t = x_all.shape[0]
    n = w_in_r.shape[1]
    return pl.pallas_call(
        _inproj_kernel,
        grid=(t // tt, n // nt),
        in_specs=[
            pl.BlockSpec((tt, D_MODEL), lambda i, j: (i, 0)),
            pl.BlockSpec((1, D_MODEL), lambda i, j: (0, 0)),
            pl.BlockSpec((D_MODEL, nt), lambda i, j: (0, j)),
        ],
        out_specs=pl.BlockSpec((tt, nt), lambda i, j: (i, j)),
        out_shape=jax.ShapeDtypeStruct((t, n), F32),
        scratch_shapes=[pltpu.VMEM((tt, D_MODEL), BF16)],
        compiler_params=_cparams(("arbitrary", "arbitrary")),
        name="inproj",
    )(x_all, norm_w, w_in_r)


def _silu(v):
    return v * jax.nn.sigmoid(v)


def _mixer_block(a_b, u, z, xbc, dt_raw, r0, su_ref, sx_ref, s_ref, gated_ref,
                 caw_ref, scw_ref, scb_ref, dtb_ref, alog_ref, dexp_ref, nrm_ref):
    R = ROWS
    su_ref[8:8 + R, :] = u
    u1 = su_ref[7:7 + R, :]
    u2 = su_ref[6:6 + R, :]
    conv_a = caw_ref[2:3, :] * u + caw_ref[1:2, :] * u1 + caw_ref[0:1, :] * u2
    ya = (a_b * conv_a).astype(BF16)
    su_ref[0:8, :] = su_ref[R:R + 8, :]

    sx_ref[8:8 + R, :] = xbc
    x1 = sx_ref[7:7 + R, :]
    x2 = sx_ref[6:6 + R, :]
    x3 = sx_ref[5:5 + R, :]
    xc = (scw_ref[3:4, :] * xbc + scw_ref[2:3, :] * x1 + scw_ref[1:2, :] * x2
          + scw_ref[0:1, :] * x3 + scb_ref[...])
    xc = _silu(xc)
    sx_ref[0:8, :] = sx_ref[R:R + 8, :]

    row = lax.broadcasted_iota(jnp.int32, (R, LANES), 0)
    col = lax.broadcasted_iota(jnp.int32, (R, LANES), 1)
    valid = row >= r0
    dt = jnp.where(valid, jax.nn.softplus(dt_raw + dtb_ref[...]), 0.0)
    a = -jnp.exp(alog_ref[...])
    da = dt * a
    tri = row >= col
    tri_f = tri.astype(F32)
    cum = jnp.dot(tri_f, da, preferred_element_type=F32, precision=lax.Precision.HIGHEST)
    da_t = da.T
    dt_t = dt.T
    triu_f = (row <= col).astype(F32)
    cum_t = jnp.dot(da_t, triu_f, preferred_element_type=F32, precision=lax.Precision.HIGHEST)
    ecum = jnp.exp(cum)
    cum_last = cum[R - 1:R, :]
    wend = jnp.exp(cum_last - cum) * dt
    dec_h = jnp.broadcast_to(jnp.exp(cum_t[:, R - 1:R]), (LANES, LANES))

    lane_lo = col < SSD_HEADDIM
    for g in range(SSD_GROUPS):
        b_g = xc[:, SSD_INNER + g * SSD_STATE: SSD_INNER + (g + 1) * SSD_STATE]
        c_g = xc[:, SSD_INNER + SSD_GROUPS * SSD_STATE + g * SSD_STATE:
                 SSD_INNER + SSD_GROUPS * SSD_STATE + (g + 1) * SSD_STATE]
        b_bf = b_g.astype(BF16)
        c_bf = c_g.astype(BF16)
        cb = lax.dot_general(c_bf, b_bf, (((1,), (1,)), ((), ())), preferred_element_type=F32)
        s_g = s_ref[g * 512:(g + 1) * 512, :]
        yo_g = lax.dot_general(c_bf, s_g.astype(BF16), (((1,), (1,)), ((), ())),
                               preferred_element_type=F32)
        for pr in range(SSD_HPG // 2):
            h0 = g * SSD_HPG + 2 * pr
            h1 = h0 + 1
            c0 = (h0 * SSD_HEADDIM)
            x_pair = xc[:, c0:c0 + LANES]
            lmats = []
            for h in (h0, h1):
                seg = jnp.where(tri, cum[:, h:h + 1] - cum_t[h:h + 1, :], NEG_INF)
                lmats.append((cb * jnp.exp(seg) * dt_t[h:h + 1, :]).astype(BF16))
            x_lo = jnp.where(lane_lo, x_pair, 0.0).astype(BF16)
            x_hi = jnp.where(lane_lo, 0.0, x_pair).astype(BF16)
            yd = (jnp.dot(lmats[0], x_lo, preferred_element_type=F32)
                  + jnp.dot(lmats[1], x_hi, preferred_element_type=F32))
            scal = jnp.where(lane_lo, ecum[:, h0:h0 + 1], ecum[:, h1:h1 + 1])
            yo = yo_g[:, 2 * pr * SSD_HEADDIM: 2 * pr * SSD_HEADDIM + LANES]
            y = yd + yo * scal + dexp_ref[:, c0:c0 + LANES] * x_pair
            zz = z[:, c0:c0 + LANES]
            gated_ref[:, c0:c0 + LANES] = y * _silu(zz)
            wsel = jnp.where(lane_lo, wend[:, h0:h0 + 1], wend[:, h1:h1 + 1])
            xw_t = (x_pair * wsel).T.astype(BF16)
            cs = jnp.dot(xw_t, b_bf, preferred_element_type=F32)
            dec = jnp.concatenate(
                [jnp.broadcast_to(dec_h[h0:h0 + 1, :], (SSD_HEADDIM, LANES)),
                 jnp.broadcast_to(dec_h[h1:h1 + 1, :], (SSD_HEADDIM, LANES))], axis=0)
            r_lo = h0 * SSD_HEADDIM
            s_ref[r_lo:r_lo + LANES, :] = dec * s_ref[r_lo:r_lo + LANES, :] + cs

    outs = []
    for g in range(SSD_GROUPS):
        gg = gated_ref[:, g * 512:(g + 1) * 512]
        ms = jnp.mean(gg * gg, axis=-1, keepdims=True)
        outs.append(((gg * lax.rsqrt(ms + NORM_EPS)).astype(F32)
                     * nrm_ref[:, g * 512:(g + 1) * 512]).astype(BF16))
    yn = jnp.concatenate(outs, axis=1)
    return ya, yn


def _mixer_prompt_kernel(main_ref, dt_ref, caw_ref, scw_ref, scb_ref, dtb_ref, alog_ref,
                         dexp_ref, nrm_ref,
                         ya_ref, yn_ref, nca_ref, ncs_ref, nss_ref,
                         su_ref, sx_ref, s_ref, gated_ref):
    j = pl.program_id(1)

    @pl.when(j == 0)
    def _():
        su_ref[0:8, :] = jnp.zeros((8, D_MODEL), F32)
        sx_ref[0:8, :] = jnp.zeros((8, SSD_CONV_DIM), F32)
        s_ref[...] = jnp.zeros_like(s_ref)

    a_b = main_ref[:, COL_AB:COL_AB + 1024]
    u = main_ref[:, COL_AC:COL_AC + 1024] * main_ref[:, COL_AX:COL_AX + 1024]
    z = main_ref[:, COL_Z:COL_Z + 2048]
    xbc = main_ref[:, COL_XBC:COL_XBC + SSD_CONV_DIM]
    r0 = jnp.where(j == 0, ROWS - N_META, 0)
    ya, yn = _mixer_block(a_b, u, z, xbc, dt_ref[...], r0, su_ref, sx_ref, s_ref, gated_ref,
                          caw_ref, scw_ref, scb_ref, dtb_ref, alog_ref, dexp_ref, nrm_ref)
    ya_ref[...] = ya
    yn_ref[...] = yn
    nca_ref[0] = su_ref[6:8, :]
    ncs_ref[0] = sx_ref[5:8, :]
    nss_ref[0] = s_ref[...]


def _mixer_sample_kernel(main_ref, dt_ref, ca0_ref, cs0_ref, ss0_ref,
                         caw_ref, scw_ref, scb_ref, dtb_ref, alog_ref, dexp_ref, nrm_ref,
                         ya_in, yn_in,
                         ya_ref, yn_ref, nca_ref, ncs_ref, nss_ref,
                         su_ref, sx_ref, s_ref, gated_ref, pad_ref, tmp_ref):
    del ya_in, yn_in
    R = ROWS
    n = 8
    su_ref[0:8, :] = jnp.zeros((8, D_MODEL), F32)
    sx_ref[0:8, :] = jnp.zeros((8, SSD_CONV_DIM), F32)
    s_ref[...] = ss0_ref[0]

    pad_ref[...] = jnp.zeros_like(pad_ref)
    pad_ref[R - n:R, :] = main_ref[...]
    a_b = pad_ref[:, COL_AB:COL_AB + 1024]
    z = pad_ref[:, COL_Z:COL_Z + 2048]
    tmp_ref[...] = jnp.zeros_like(tmp_ref)
    tmp_ref[6:8, 0:D_MODEL] = ca0_ref[0]
    u = jnp.concatenate(
        [jnp.zeros((R - 2 * n, D_MODEL), F32), tmp_ref[:, 0:D_MODEL],
         main_ref[:, COL_AC:COL_AC + 1024] * main_ref[:, COL_AX:COL_AX + 1024]], axis=0)
    tmp_ref[...] = jnp.zeros_like(tmp_ref)
    tmp_ref[5:8, :] = cs0_ref[0]
    xbc = jnp.concatenate(
        [jnp.zeros((R - 2 * n, SSD_CONV_DIM), F32), tmp_ref[...],
         main_ref[:, COL_XBC:COL_XBC + SSD_CONV_DIM]], axis=0)
    dt_raw = jnp.concatenate([jnp.zeros((R - n, LANES), F32), dt_ref[...]], axis=0)
    ya, yn = _mixer_block(a_b, u, z, xbc, dt_raw, R - n, su_ref, sx_ref, s_ref, gated_ref,
                          caw_ref, scw_ref, scb_ref, dtb_ref, alog_ref, dexp_ref, nrm_ref)
    ya_ref[...] = ya[R - n:R, :]
    yn_ref[...] = yn[R - n:R, :]
    nca_ref[0] = su_ref[6:8, :]
    ncs_ref[0] = sx_ref[5:8, :]
    nss_ref[0] = s_ref[...]


def _mixer_weight_specs(nidx):
    zero = (lambda *a: (0, 0))
    del nidx
    return [
        pl.BlockSpec((CONV_A_W, D_MODEL), zero),
        pl.BlockSpec((SSD_CONV_W, SSD_CONV_DIM), zero),
        pl.BlockSpec((1, SSD_CONV_DIM), zero),
        pl.BlockSpec((1, LANES), zero),
        pl.BlockSpec((1, LANES), zero),
        pl.BlockSpec((1, SSD_INNER), zero),
        pl.BlockSpec((1, SSD_INNER), zero),
    ]


def _mixer_scratch():
    return [
        pltpu.VMEM((ROWS + 8, D_MODEL), F32),
        pltpu.VMEM((ROWS + 8, SSD_CONV_DIM), F32),
        pltpu.VMEM((SSD_INNER, SSD_STATE), F32),
        pltpu.VMEM((ROWS, SSD_INNER), F32),
    ]


def _mixer_prompt(proj, weights, nseq, nblk, t_all):
    dt_blk = COL_DT // LANES
    return pl.pallas_call(
        _mixer_prompt_kernel,
        grid=(nseq, nblk),
        in_specs=[
            pl.BlockSpec((ROWS, MAIN_W), lambda b, j: (b * nblk + j, 0)),
            pl.BlockSpec((ROWS, LANES), lambda b, j: (b * nblk + j, dt_blk)),
        ] + _mixer_weight_specs(2),
        out_specs=[
            pl.BlockSpec((ROWS, D_MODEL), lambda b, j: (b * nblk + j, 0)),
            pl.BlockSpec((ROWS, SSD_INNER), lambda b, j: (b * nblk + j, 0)),
            pl.BlockSpec((1, CONV_A_W - 1, D_MODEL), lambda b, j: (b, 0, 0)),
            pl.BlockSpec((1, SSD_CONV_W - 1, SSD_CONV_DIM), lambda b, j: (b, 0, 0)),
            pl.BlockSpec((1, SSD_INNER, SSD_STATE), lambda b, j: (b, 0, 0)),
        ],
        out_shape=[
            jax.ShapeDtypeStruct((t_all, D_MODEL), BF16),
            jax.ShapeDtypeStruct((t_all, SSD_INNER), BF16),
            jax.ShapeDtypeStruct((nseq, CONV_A_W - 1, D_MODEL), F32),
            jax.ShapeDtypeStruct((nseq, SSD_CONV_W - 1, SSD_CONV_DIM), F32),
            jax.ShapeDtypeStruct((nseq, SSD_INNER, SSD_STATE), F32),
        ],
        scratch_shapes=_mixer_scratch(),
        compiler_params=_cparams(("arbitrary", "arbitrary")),
        name="mixer_prompt",
    )(proj, proj, *weights)


def _mixer_sample(proj, ca0, cs0, ss0, weights, ya_all, yn_all, nseq, row0):
    dt_blk = COL_DT // LANES
    blk0 = row0 // 8
    return pl.pallas_call(
        _mixer_sample_kernel,
        grid=(nseq,),
        in_specs=[
            pl.BlockSpec((8, MAIN_W), lambda s: (blk0 + s, 0)),
            pl.BlockSpec((8, LANES), lambda s: (blk0 + s, dt_blk)),
            pl.BlockSpec((1, CONV_A_W - 1, D_MODEL), lambda s: (s, 0, 0)),
            pl.BlockSpec((1, SSD_CONV_W - 1, SSD_CONV_DIM), lambda s: (s, 0, 0)),
            pl.BlockSpec((1, SSD_INNER, SSD_STATE), lambda s: (s, 0, 0)),
        ] + _mixer_weight_specs(1) + [
            pl.BlockSpec(memory_space=pl.ANY),
            pl.BlockSpec(memory_space=pl.ANY),
        ],
        out_specs=[
            pl.BlockSpec((8, D_MODEL), lambda s: (blk0 + s, 0)),
            pl.BlockSpec((8, SSD_INNER), lambda s: (blk0 + s, 0)),
            pl.BlockSpec((1, CONV_A_W - 1, D_MODEL), lambda s: (s, 0, 0)),
            pl.BlockSpec((1, SSD_CONV_W - 1, SSD_CONV_DIM), lambda s: (s, 0, 0)),
            pl.BlockSpec((1, SSD_INNER, SSD_STATE), lambda s: (s, 0, 0)),
        ],
        out_shape=[
            jax.ShapeDtypeStruct(ya_all.shape, BF16),
            jax.ShapeDtypeStruct(yn_all.shape, BF16),
            jax.ShapeDtypeStruct((nseq, CONV_A_W - 1, D_MODEL), F32),
            jax.ShapeDtypeStruct((nseq, SSD_CONV_W - 1, SSD_CONV_DIM), F32),
            jax.ShapeDtypeStruct((nseq, SSD_INNER, SSD_STATE), F32),
        ],
        scratch_shapes=_mixer_scratch() + [
            pltpu.VMEM((ROWS, MAIN_W), F32),
            pltpu.VMEM((8, SSD_CONV_DIM), F32),
        ],
        input_output_aliases={12: 0, 13: 1},
        compiler_params=_cparams(("arbitrary",)),
        name="mixer_sample",
    )(proj, proj, ca0, cs0, ss0, *weights, ya_all, yn_all)


def _merge_kernel(x_ref, ya_ref, yn_ref, g_ref, wa_ref, ws_ref, wo_ref, nw_ref, x2_ref, hn_ref):
    br_a = jnp.dot(ya_ref[...], wa_ref[...], preferred_element_type=F32)
    br_b = jnp.dot(yn_ref[...], ws_ref[...], preferred_element_type=F32)
    g = g_ref[...]
    merged = jax.nn.sigmoid(g[:, :D_MODEL]) * br_a + jax.nn.sigmoid(g[:, D_MODEL:]) * br_b
    mix = jnp.dot(merged.astype(BF16), wo_ref[...], preferred_element_type=F32)
    x2 = x_ref[...] + mix
    x2_ref[...] = x2
    inv = lax.rsqrt(jnp.mean(x2 * x2, axis=-1, keepdims=True) + NORM_EPS)
    hn_ref[...] = ((x2 * inv) * nw_ref[...]).T.astype(BF16)


def _merge(x_all, ya, yn, proj, wa, ws, wo, nw, tt):
    t = x_all.shape[0]
    gblk = COL_G // (2 * D_MODEL)
    zero = lambda i: (0, 0)
    return pl.pallas_call(
        _merge_kernel,
        grid=(t // tt,),
        in_specs=[
            pl.BlockSpec((tt, D_MODEL), lambda i: (i, 0)),
            pl.BlockSpec((tt, D_MODEL), lambda i: (i, 0)),
            pl.BlockSpec((tt, SSD_INNER), lambda i: (i, 0)),
            pl.BlockSpec((tt, 2 * D_MODEL), lambda i: (i, gblk)),
            pl.BlockSpec((D_MODEL, D_MODEL), zero),
            pl.BlockSpec((SSD_INNER, D_MODEL), zero),
            pl.BlockSpec((D_MODEL, D_MODEL), zero),
            pl.BlockSpec((1, D_MODEL), zero),
        ],
        out_specs=[
            pl.BlockSpec((tt, D_MODEL), lambda i: (i, 0)),
            pl.BlockSpec((D_MODEL, tt), lambda i: (0, i)),
        ],
        out_shape=[
            jax.ShapeDtypeStruct((t, D_MODEL), F32),
            jax.ShapeDtypeStruct((D_MODEL, t), BF16),
        ],
        compiler_params=_cparams(("arbitrary",)),
        name="merge",
    )(x_all, ya, yn, proj, wa, ws, wo, nw)


def _oddeven_merge_sort_pairs(n):
    pairs = []
    p = 1
    while p < n:
        k = p
        while k >= 1:
            j = k % p
            while j <= n - 1 - k:
                for i in range(min(k, n - j - k)):
                    if (i + j) // (2 * p) == (i + j + k) // (2 * p):
                        pairs.append((i + j, i + j + k))
                j += 2 * k
            k //= 2
        p *= 2
    return pairs


_SORT16 = _oddeven_merge_sort_pairs(16)
_BITONIC16 = [(i, i + d) for d in (8, 4, 2, 1) for i in range(16) if (i & d) == 0]
_CAND_PAIRS = [(r1, r2) for r1 in range(PEER_TOPK) for r2 in range(PEER_TOPK)
               if (r1 + 1) * (r2 + 1) <= PEER_TOPK]


def _ce(v, i, j):
    hi = jnp.maximum(v[i], v[j])
    lo = jnp.minimum(v[i], v[j])
    v[i], v[j] = hi, lo


def _top16_desc(v):
    v = list(v)
    for i, j in _SORT16:
        _ce(v, i, j)
    for shift in (4, 2, 1):
        p = [pltpu.roll(x, shift, 0) for x in v]
        v = [jnp.maximum(v[r], p[15 - r]) for r in range(16)]
        for i, j in _BITONIC16:
            _ce(v, i, j)
    return v


def _route_kernel(ht_ref, wq_ref, keys_ref, s_ref, e_ref, tau_ref, zinv_ref, q_ref):
    tt = ht_ref.shape[1]
    q_ref[...] = jnp.dot(wq_ref[...], ht_ref[...], preferred_element_type=F32).astype(BF16)
    sub = lax.broadcasted_iota(jnp.int32, (SUBLANES, tt), 0)
    tops = [[None] * PEER_TOPK, [None] * PEER_TOPK]
    for h in range(PEER_HEADS):
        for i in range(2):
            r0 = (h * 2 + i) * PEER_HALF
            s = jnp.dot(keys_ref[h * 2 + i], q_ref[r0:r0 + PEER_HALF, :],
                        preferred_element_type=F32)
            v = [s[8 * j:8 * j + 8, :] for j in range(16)]
            top = _top16_desc(v)
            thr, best = top[PEER_TOPK - 1], top[0]
            for j in range(16):
                s_ref[i, h, 8 * j:8 * j + 8, :] = jnp.where(v[j] >= thr, v[j], NEG_INF)
                e_ref[i, h, 8 * j:8 * j + 8, :] = jnp.exp(v[j] - best)
            for r in range(PEER_TOPK):
                tops[i][r] = top[r] if h == 0 else jnp.where(sub == h, top[r], tops[i][r])
    cand = [tops[0][r1] + tops[1][r2] for r1, r2 in _CAND_PAIRS]
    picked = []
    for _ in range(PEER_TOPK):
        m = cand[0]
        for c in cand[1:]:
            m = jnp.maximum(m, c)
        picked.append(m)
        found = jnp.zeros(m.shape, jnp.bool_)
        nxt = []
        for c in cand:
            eq = c == m
            take = jnp.logical_and(eq, jnp.logical_not(found))
            found = jnp.logical_or(found, eq)
            nxt.append(jnp.where(take, NEG_INF, c))
        cand = nxt
    zsum = jnp.zeros_like(picked[0])
    for c in picked:
        zsum = zsum + jnp.exp(c - picked[0])
    tau_ref[...] = picked[PEER_TOPK - 1]
    zinv_ref[...] = 1.0 / zsum


def _route(hn_t, wq_t, keys, tt):
    t = hn_t.shape[1]
    return pl.pallas_call(
        _route_kernel,
        grid=(t // tt,),
        in_specs=[
            pl.BlockSpec((D_MODEL, tt), lambda i: (0, i)),
            pl.BlockSpec(wq_t.shape, lambda i: (0, 0)),
            pl.BlockSpec(keys.shape, lambda i: (0, 0, 0)),
        ],
        out_specs=[
            pl.BlockSpec((2, PEER_HEADS, N_KEYS, tt), lambda i: (0, 0, 0, i)),
            pl.BlockSpec((2, PEER_HEADS, N_KEYS, tt), lambda i: (0, 0, 0, i)),
            pl.BlockSpec((PEER_HEADS, tt), lambda i: (0, i)),
            pl.BlockSpec((PEER_HEADS, tt), lambda i: (0, i)),
        ],
        out_shape=[
            jax.ShapeDtypeStruct((2, PEER_HEADS, N_KEYS, t), F32),
            jax.ShapeDtypeStruct((2, PEER_HEADS, N_KEYS, t), F32),
            jax.ShapeDtypeStruct((PEER_HEADS, t), F32),
            jax.ShapeDtypeStruct((PEER_HEADS, t), F32),
        ],
        scratch_shapes=[pltpu.VMEM((wq_t.shape[0], tt), BF16)],
        compiler_params=_cparams(("arbitrary",)),
        name="peer_route",
    )(hn_t, wq_t, keys)


I1_PER_STEP = 8
EXPERTS_PER_STEP = I1_PER_STEP * N_KEYS


def _expert_kernel(ht_ref, u_ref, vt_ref, s1_ref, e1_ref, s2_ref, e2_ref, tau_ref, zinv_ref,
                   x2_ref, nfw_ref, y_ref, acc_ref, w_ref):
    c = pl.program_id(1)
    tt = ht_ref.shape[1]

    @pl.when(c == 0)
    def _():
        acc_ref[...] = jnp.zeros_like(acc_ref)

    for j in range(I1_PER_STEP):
        act = jnp.dot(u_ref[j * N_KEYS:(j + 1) * N_KEYS, :], ht_ref[...],
                      preferred_element_type=F32)
        for half in range(tt // LANES):
            ls = pl.ds(half * LANES, LANES)
            taub = [jnp.broadcast_to(tau_ref[h:h + 1, ls], (SUBLANES, LANES))
                    for h in range(PEER_HEADS)]
            s1b = [jnp.broadcast_to(s1_ref[h, 0, j:j + 1, ls], (SUBLANES, LANES))
                   for h in range(PEER_HEADS)]
            e1b = [jnp.broadcast_to(e1_ref[h, 0, j:j + 1, ls] * zinv_ref[h:h + 1, ls],
                                    (SUBLANES, LANES)) for h in range(PEER_HEADS)]
            for bb in range(N_KEYS // BF16_ROWS):
                gs = []
                for sb in range(BF16_ROWS // SUBLANES):
                    r = bb * BF16_ROWS + sb * SUBLANES
                    g = None
                    for h in range(PEER_HEADS):
                        v = s1b[h] + s2_ref[h, r:r + SUBLANES, ls]
                        t = jnp.where(v >= taub[h], e1b[h] * e2_ref[h, r:r + SUBLANES, ls], 0.0)
                        g = t if g is None else g + t
                    gs.append(g)
                a = act[bb * BF16_ROWS:(bb + 1) * BF16_ROWS, half * LANES:(half + 1) * LANES]
                ge = 0.5 * a * (1.0 + lax.erf(a * math.sqrt(0.5)))
                r0 = j * N_KEYS + bb * BF16_ROWS
                w_ref[r0:r0 + BF16_ROWS, ls] = (ge * jnp.concatenate(gs, axis=0)).astype(BF16)
        if j % 4 == 3:
            k0 = (j - 3) * N_KEYS
            acc_ref[...] += jnp.dot(vt_ref[:, k0:k0 + 4 * N_KEYS], w_ref[k0:k0 + 4 * N_KEYS, :],
                                    preferred_element_type=F32)

    @pl.when(c == pl.num_programs(1) - 1)
    def _():
        x3 = x2_ref[...] + acc_ref[...].T
        inv = lax.rsqrt(jnp.mean(x3 * x3, axis=-1, keepdims=True) + NORM_EPS)
        y_ref[...] = (x3 * inv) * nfw_ref[...]


def _experts(hn_t, u_bf, v_t, s_m, e_x, tau, zinv, x2, nfw, tt):
    t = hn_t.shape[1]
    n_exp = u_bf.shape[0]
    nc = n_exp // EXPERTS_PER_STEP
    s1 = s_m[0].reshape(PEER_HEADS, nc, I1_PER_STEP, t)
    e1 = e_x[0].reshape(PEER_HEADS, nc, I1_PER_STEP, t)
    s2 = s_m[1]
    e2 = e_x[1]
    return pl.pallas_call(
        _expert_kernel,
        grid=(t // tt, nc),
        in_specs=[
            pl.BlockSpec((D_MODEL, tt), lambda i, c: (0, i)),
            pl.BlockSpec((EXPERTS_PER_STEP, D_MODEL), lambda i, c: (c, 0)),
            pl.BlockSpec((D_MODEL, EXPERTS_PER_STEP), lambda i, c: (0, c)),
            pl.BlockSpec((PEER_HEADS, 1, I1_PER_STEP, tt), lambda i, c: (0, c, 0, i)),
            pl.BlockSpec((PEER_HEADS, 1, I1_PER_STEP, tt), lambda i, c: (0, c, 0, i)),
            pl.BlockSpec((PEER_HEADS, N_KEYS, tt), lambda i, c: (0, 0, i)),
            pl.BlockSpec((PEER_HEADS, N_KEYS, tt), lambda i, c: (0, 0, i)),
            pl.BlockSpec((PEER_HEADS, tt), lambda i, c: (0, i)),
            pl.BlockSpec((PEER_HEADS, tt), lambda i, c: (0, i)),
            pl.BlockSpec((tt, D_MODEL), lambda i, c: (i, 0)),
            pl.BlockSpec((1, D_MODEL), lambda i, c: (0, 0)),
        ],
        out_specs=pl.BlockSpec((tt, D_MODEL), lambda i, c: (i, 0)),
        out_shape=jax.ShapeDtypeStruct((t, D_MODEL), F32),
        scratch_shapes=[
            pltpu.VMEM((D_MODEL, tt), F32),
            pltpu.VMEM((EXPERTS_PER_STEP, tt), BF16),
        ],
        compiler_params=_cparams(("arbitrary", "arbitrary")),
        name="peer_experts",
    )(hn_t, u_bf, v_t, s1, e1, s2, e2, tau, zinv, x2, nfw)


def _pick_tile(t, prefs):
    for p in prefs:
        if t % p == 0:
            return p
    raise ValueError(f"token count {t} not tileable by {prefs}")


def _layer(x_all, nseq_p, nblk, ca0, cs0, ss0, lw, norm_final_w):
    (norm_mix_w, w_in, conv_a_w, w_a_out, ssd_conv_w, ssd_conv_b, ssd_dt_bias, ssd_a_log, ssd_d,
     ssd_norm_w, w_ssd_out, w_o, norm_ffn_w, peer_w_q, peer_sub_keys, peer_u, peer_v) = lw
    t_all = x_all.shape[0]
    t_p = nseq_p * nblk * ROWS
    nseq_s = ca0.shape[0]

    c = np.cumsum([0, 1024, 1024, 1024, SSD_INNER, SSD_CONV_DIM, SSD_HEADS, 1024, 1024])
    w_in_r = jnp.concatenate(
        [w_in[:, c[0]:c[5]], w_in[:, c[6]:c[8]], w_in[:, c[5]:c[6]],
         jnp.zeros((D_MODEL, LANES - SSD_HEADS), w_in.dtype)], axis=1).astype(BF16)
    pad_h = lambda v: jnp.pad(v.astype(F32), (0, LANES - SSD_HEADS)).reshape(1, LANES)
    mixer_w = (conv_a_w.astype(F32), ssd_conv_w.astype(F32), ssd_conv_b.reshape(1, -1).astype(F32),
               pad_h(ssd_dt_bias), pad_h(ssd_a_log),
               jnp.repeat(ssd_d.astype(F32), SSD_HEADDIM).reshape(1, SSD_INNER),
               ssd_norm_w.reshape(1, SSD_INNER).astype(F32))

    proj = _inproj(x_all, norm_mix_w.reshape(1, D_MODEL), w_in_r,
                   _pick_tile(t_all, (1024, 512, 256)), 1152)
    ya, yn, nca_p, ncs_p, nss_p = _mixer_prompt(proj, mixer_w, nseq_p, nblk, t_all)
    ya, yn, nca_s, ncs_s, nss_s = _mixer_sample(
        proj, ca0, cs0, ss0.reshape(nseq_s, SSD_INNER, SSD_STATE), mixer_w, ya, yn, nseq_s, t_p)

    x2, hn_t = _merge(x_all, ya, yn, proj, w_a_out.astype(BF16), w_ssd_out.astype(BF16),
                      w_o.astype(BF16), norm_ffn_w.reshape(1, D_MODEL),
                      _pick_tile(t_all, (512, 256)))

    wq_t = peer_w_q.T.astype(BF16)
    keys = peer_sub_keys.reshape(PEER_HEADS * 2, N_KEYS, PEER_HALF).astype(BF16)
    tt = _pick_tile(t_all, (256, 128))
    s_m, e_x, tau, zinv = _route(hn_t, wq_t, keys, tt)
    y = _experts(hn_t, peer_u.astype(BF16), peer_v.T.astype(BF16), s_m, e_x, tau, zinv, x2,
                 norm_final_w.reshape(1, D_MODEL), tt)
    shp = (SSD_HEADS, SSD_HEADDIM, SSD_STATE)
    return (y, nca_p, ncs_p, nss_p.reshape(nseq_p, *shp), nca_s, ncs_s, nss_s.reshape(nseq_s, *shp))


def kernel(x_prompt, x_sample, state_conv_a, state_conv_ssd, state_ssm, meta_tokens, norm_mix_w, w_in, conv_a_w, w_a_out, ssd_conv_w, ssd_conv_b, ssd_dt_bias, ssd_a_log, ssd_d, ssd_norm_w, w_ssd_out, w_o, norm_ffn_w, peer_w_q, peer_sub_keys, peer_u, peer_v, norm_final_w):
    bp, s_len, d = x_prompt.shape
    bs, ds_len, _ = x_sample.shape
    depth = w_in.shape[0]
    assert depth == 1 and d == D_MODEL and ds_len == 8 and s_len % ROWS == 0
    nblk = s_len // ROWS + 1
    head = jnp.concatenate(
        [jnp.zeros((ROWS - N_META, d), x_prompt.dtype), meta_tokens.astype(x_prompt.dtype)], axis=0)
    xp = jnp.concatenate([jnp.broadcast_to(head[None], (bp, ROWS, d)), x_prompt], axis=1)
    x_all = jnp.concatenate([xp.reshape(-1, d), x_sample.reshape(-1, d)], axis=0)
    lw = (norm_mix_w[0], w_in[0], conv_a_w[0], w_a_out[0], ssd_conv_w[0], ssd_conv_b[0],
          ssd_dt_bias[0], ssd_a_log[0], ssd_d[0], ssd_norm_w[0], w_ssd_out[0], w_o[0],
          norm_ffn_w[0], peer_w_q[0], peer_sub_keys[0], peer_u[0], peer_v[0])
    y, nca_p, ncs_p, nss_p, nca_s, ncs_s, nss_s = _layer(
        x_all, bp, nblk, state_conv_a[0], state_conv_ssd[0], state_ssm[0], lw, norm_final_w)
    t_p = bp * nblk * ROWS
    y_prompt = y[:t_p].reshape(bp, nblk * ROWS, d)[:, ROWS:]
    y_sample = y[t_p:].reshape(bs, ds_len, d)
    return (y_prompt, y_sample, nca_p[None], ncs_p[None], nss_p[None],
            nca_s[None], ncs_s[None], nss_s[None])
```

```python
import functools
import math

import numpy as np
import jax
import jax.numpy as jnp
from jax import lax
from jax.experimental import pallas as pl
from jax.experimental.pallas import tpu as pltpu

F32 = jnp.float32
BF16 = jnp.bfloat16

D_MODEL = 1024
N_META = 16
NORM_EPS = 1e-6
CONV_A_W = 3
SSD_INNER = 2048
SSD_HEADDIM = 64
SSD_HEADS = 32
SSD_STATE = 128
SSD_GROUPS = 4
SSD_HPG = 8
SSD_CONV_W = 4
SSD_CONV_DIM = SSD_INNER + 2 * SSD_GROUPS * SSD_STATE
PEER_HEADS = 8
N_KEYS = 128
PEER_TOPK = 16
PEER_HALF = 128

ROWS = 128
LANES = 128
SUBLANES = 8
BF16_ROWS = 16
COL_AB, COL_AC, COL_AX, COL_Z, COL_XBC = 0, 1024, 2048, 3072, 5120
COL_G = 8192
COL_DT = 10240
IN_DIM_R = COL_DT + LANES
MAIN_W = COL_G
NEG_INF = float("-inf")

VMEM_LIMIT = 56 * 1024 * 1024


def _cparams(sem, flags=None):
    return pltpu.CompilerParams(dimension_semantics=sem, vmem_limit_bytes=VMEM_LIMIT, flags=flags)


def _inproj_kernel(x_ref, nw_ref, w_ref, o_ref, hn_ref):
    @pl.when(pl.program_id(1) == 0)
    def _():
        x = x_ref[...]
        inv = lax.rsqrt(jnp.mean(x * x, axis=-1, keepdims=True) + NORM_EPS)
        hn_ref[...] = ((x * inv) * nw_ref[...]).astype(BF16)

    o_ref[...] = jnp.dot(hn_ref[...], w_ref[...], preferred_element_type=F32)


def _inproj(x_all, norm_w, w_in_r, tt, nt):
    t = x_all.shape[0]
    n = w_in_r.shape[1]
    return pl.pallas_call(
        _inproj_kernel,
        grid=(t // tt, n // nt),
        in_specs=[
            pl.BlockSpec((tt, D_MODEL), lambda i, j: (i, 0)),
            pl.BlockSpec((1, D_MODEL), lambda i, j: (0, 0)),
            pl.BlockSpec((D_MODEL, nt), lambda i, j: (0, j)),
        ],
        out_specs=pl.BlockSpec((tt, nt), lambda i, j: (i, j)),
        out_shape=jax.ShapeDtypeStruct((t, n), F32),
        scratch_shapes=[pltpu.VMEM((tt, D_MODEL), BF16)],
        compiler_params=_cparams(("arbitrary", "arbitrary")),
        name="inproj",
    )(x_all, norm_w, w_in_r)


def _silu(v):
    return v * jax.nn.sigmoid(v)


def _mixer_block(a_b, u, z, xbc, dt_raw, r0, su_ref, sx_ref, s_ref, gated_ref,
                 caw_ref, scw_ref, scb_ref, dtb_ref, alog_ref, dexp_ref, nrm_ref):
    R = ROWS
    su_ref[8:8 + R, :] = u
    u1 = su_ref[7:7 + R, :]
    u2 = su_ref[6:6 + R, :]
    conv_a = caw_ref[2:3, :] * u + caw_ref[1:2, :] * u1 + caw_ref[0:1, :] * u2
    ya = (a_b * conv_a).astype(BF16)
    su_ref[0:8, :] = su_ref[R:R + 8, :]

    sx_ref[8:8 + R, :] = xbc
    x1 = sx_ref[7:7 + R, :]
    x2 = sx_ref[6:6 + R, :]
    x3 = sx_ref[5:5 + R, :]
    xc = (scw_ref[3:4, :] * xbc + scw_ref[2:3, :] * x1 + scw_ref[1:2, :] * x2
          + scw_ref[0:1, :] * x3 + scb_ref[...])
    xc = _silu(xc)
    sx_ref[0:8, :] = sx_ref[R:R + 8, :]

    row = lax.broadcasted_iota(jnp.int32, (R, LANES), 0)
    col = lax.broadcasted_iota(jnp.int32, (R, LANES), 1)
    valid = row >= r0
    dt = jnp.where(valid, jax.nn.softplus(dt_raw + dtb_ref[...]), 0.0)
    a = -jnp.exp(alog_ref[...])
    da = dt * a
    tri = row >= col
    tri_f = tri.astype(F32)
    cum = jnp.dot(tri_f, da, preferred_element_type=F32, precision=lax.Precision.HIGHEST)
    da_t = da.T
    dt_t = dt.T
    triu_f = (row <= col).astype(F32)
    cum_t = jnp.dot(da_t, triu_f, preferred_element_type=F32, precision=lax.Precision.HIGHEST)
    ecum = jnp.exp(cum)
    cum_last = cum[R - 1:R, :]
    wend = jnp.exp(cum_last - cum) * dt
    dec_h = jnp.broadcast_to(jnp.exp(cum_t[:, R - 1:R]), (LANES, LANES))

    lane_lo = col < SSD_HEADDIM
    for g in range(SSD_GROUPS):
        b_g = xc[:, SSD_INNER + g * SSD_STATE: SSD_INNER + (g + 1) * SSD_STATE]
        c_g = xc[:, SSD_INNER + SSD_GROUPS * SSD_STATE + g * SSD_STATE:
                 SSD_INNER + SSD_GROUPS * SSD_STATE + (g + 1) * SSD_STATE]
        b_bf = b_g.astype(BF16)
        c_bf = c_g.astype(BF16)
        cb = lax.dot_general(c_bf, b_bf, (((1,), (1,)), ((), ())), preferred_element_type=F32)
        s_g = s_ref[g * 512:(g + 1) * 512, :]
        yo_g = lax.dot_general(c_bf, s_g.astype(BF16), (((1,), (1,)), ((), ())),
                               preferred_element_type=F32)
        for pr in range(SSD_HPG // 2):
            h0 = g * SSD_HPG + 2 * pr
            h1 = h0 + 1
            c0 = (h0 * SSD_HEADDIM)
            x_pair = xc[:, c0:c0 + LANES]
            lmats = []
            for h in (h0, h1):
                seg = jnp.where(tri, cum[:, h:h + 1] - cum_t[h:h + 1, :], NEG_INF)
                lmats.append((cb * jnp.exp(seg) * dt_t[h:h + 1, :]).astype(BF16))
            x_lo = jnp.where(lane_lo, x_pair, 0.0).astype(BF16)
            x_hi = jnp.where(lane_lo, 0.0, x_pair).astype(BF16)
            yd = (jnp.dot(lmats[0], x_lo, preferred_element_type=F32)
                  + jnp.dot(lmats[1], x_hi, preferred_element_type=F32))
            scal = jnp.where(lane_lo, ecum[:, h0:h0 + 1], ecum[:, h1:h1 + 1])
            yo = yo_g[:, 2 * pr * SSD_HEADDIM: 2 * pr * SSD_HEADDIM + LANES]
            y = yd + yo * scal + dexp_ref[:, c0:c0 + LANES] * x_pair
            zz = z[:, c0:c0 + LANES]
            gated_ref[:, c0:c0 + LANES] = y * _silu(zz)
            wsel = jnp.where(lane_lo, wend[:, h0:h0 + 1], wend[:, h1:h1 + 1])
            xw_t = (x_pair * wsel).T.astype(BF16)
            cs = jnp.dot(xw_t, b_bf, preferred_element_type=F32)
            dec = jnp.concatenate(
                [jnp.broadcast_to(dec_h[h0:h0 + 1, :], (SSD_HEADDIM, LANES)),
                 jnp.broadcast_to(dec_h[h1:h1 + 1, :], (SSD_HEADDIM, LANES))], axis=0)
            r_lo = h0 * SSD_HEADDIM
            s_ref[r_lo:r_lo + LANES, :] = dec * s_ref[r_lo:r_lo + LANES, :] + cs

    outs = []
    for g in range(SSD_GROUPS):
        gg = gated_ref[:, g * 512:(g + 1) * 512]
        ms = jnp.mean(gg * gg, axis=-1, keepdims=True)
        outs.append(((gg * lax.rsqrt(ms + NORM_EPS)).astype(F32)
                     * nrm_ref[:, g * 512:(g + 1) * 512]).astype(BF16))
    yn = jnp.concatenate(outs, axis=1)
    return ya, yn


def _mixer_prompt_kernel(main_ref, dt_ref, caw_ref, scw_ref, scb_ref, dtb_ref, alog_ref,
                         dexp_ref, nrm_ref,
                         ya_ref, yn_ref, nca_ref, ncs_ref, nss_ref,
                         su_ref, sx_ref, s_ref, gated_ref):
    j = pl.program_id(1)

    @pl.when(j == 0)
    def _():
        su_ref[0:8, :] = jnp.zeros((8, D_MODEL), F32)
        sx_ref[0:8, :] = jnp.zeros((8, SSD_CONV_DIM), F32)
        s_ref[...] = jnp.zeros_like(s_ref)

    a_b = main_ref[:, COL_AB:COL_AB + 1024]
    u = main_ref[:, COL_AC:COL_AC + 1024] * main_ref[:, COL_AX:COL_AX + 1024]
    z = main_ref[:, COL_Z:COL_Z + 2048]
    xbc = main_ref[:, COL_XBC:COL_XBC + SSD_CONV_DIM]
    r0 = jnp.where(j == 0, ROWS - N_META, 0)
    ya, yn = _mixer_block(a_b, u, z, xbc, dt_ref[...], r0, su_ref, sx_ref, s_ref, gated_ref,
                          caw_ref, scw_ref, scb_ref, dtb_ref, alog_ref, dexp_ref, nrm_ref)
    ya_ref[...] = ya
    yn_ref[...] = yn
    nca_ref[0] = su_ref[6:8, :]
    ncs_ref[0] = sx_ref[5:8, :]
    nss_ref[0] = s_ref[...]


def _mixer_sample_kernel(main_ref, dt_ref, ca0_ref, cs0_ref, ss0_ref,
                         caw_ref, scw_ref, scb_ref, dtb_ref, alog_ref, dexp_ref, nrm_ref,
                         ya_ref, yn_ref, nca_ref, ncs_ref, nss_ref,
                         su_ref, sx_ref, s_ref, gated_ref, pad_ref, tmp_ref):
    R = ROWS
    n = 8
    su_ref[0:8, :] = jnp.zeros((8, D_MODEL), F32)
    sx_ref[0:8, :] = jnp.zeros((8, SSD_CONV_DIM), F32)
    s_ref[...] = ss0_ref[0]

    pad_ref[...] = jnp.zeros_like(pad_ref)
    pad_ref[R - n:R, :] = main_ref[...]
    a_b = pad_ref[:, COL_AB:COL_AB + 1024]
    z = pad_ref[:, COL_Z:COL_Z + 2048]
    tmp_ref[...] = jnp.zeros_like(tmp_ref)
    tmp_ref[6:8, 0:D_MODEL] = ca0_ref[0]
    u = jnp.concatenate(
        [jnp.zeros((R - 2 * n, D_MODEL), F32), tmp_ref[:, 0:D_MODEL],
         main_ref[:, COL_AC:COL_AC + 1024] * main_ref[:, COL_AX:COL_AX + 1024]], axis=0)
    tmp_ref[...] = jnp.zeros_like(tmp_ref)
    tmp_ref[5:8, :] = cs0_ref[0]
    xbc = jnp.concatenate(
        [jnp.zeros((R - 2 * n, SSD_CONV_DIM), F32), tmp_ref[...],
         main_ref[:, COL_XBC:COL_XBC + SSD_CONV_DIM]], axis=0)
    dt_raw = jnp.concatenate([jnp.zeros((R - n, LANES), F32), dt_ref[...]], axis=0)
    ya, yn = _mixer_block(a_b, u, z, xbc, dt_raw, R - n, su_ref, sx_ref, s_ref, gated_ref,
                          caw_ref, scw_ref, scb_ref, dtb_ref, alog_ref, dexp_ref, nrm_ref)
    ya_ref[...] = ya[R - n:R, :]
    yn_ref[...] = yn[R - n:R, :]
    nca_ref[0] = su_ref[6:8, :]
    ncs_ref[0] = sx_ref[5:8, :]
    nss_ref[0] = s_ref[...]


def _mixer_weight_specs(nidx):
    zero = (lambda *a: (0, 0))
    del nidx
    return [
        pl.BlockSpec((CONV_A_W, D_MODEL), zero),
        pl.BlockSpec((SSD_CONV_W, SSD_CONV_DIM), zero),
        pl.BlockSpec((1, SSD_CONV_DIM), zero),
        pl.BlockSpec((1, LANES), zero),
        pl.BlockSpec((1, LANES), zero),
        pl.BlockSpec((1, SSD_INNER), zero),
        pl.BlockSpec((1, SSD_INNER), zero),
    ]


def _mixer_scratch():
    return [
        pltpu.VMEM((ROWS + 8, D_MODEL), F32),
        pltpu.VMEM((ROWS + 8, SSD_CONV_DIM), F32),
        pltpu.VMEM((SSD_INNER, SSD_STATE), F32),
        pltpu.VMEM((ROWS, SSD_INNER), F32),
    ]


def _mixer_prompt(proj, weights, nseq, nblk):
    dt_blk = COL_DT // LANES
    return pl.pallas_call(
        _mixer_prompt_kernel,
        grid=(nseq, nblk),
        in_specs=[
            pl.BlockSpec((ROWS, MAIN_W), lambda b, j: (b * nblk + j, 0)),
            pl.BlockSpec((ROWS, LANES), lambda b, j: (b * nblk + j, dt_blk)),
        ] + _mixer_weight_specs(2),
        out_specs=[
            pl.BlockSpec((ROWS, D_MODEL), lambda b, j: (b * nblk + j, 0)),
            pl.BlockSpec((ROWS, SSD_INNER), lambda b, j: (b * nblk + j, 0)),
            pl.BlockSpec((1, CONV_A_W - 1, D_MODEL), lambda b, j: (b, 0, 0)),
            pl.BlockSpec((1, SSD_CONV_W - 1, SSD_CONV_DIM), lambda b, j: (b, 0, 0)),
            pl.BlockSpec((1, SSD_INNER, SSD_STATE), lambda b, j: (b, 0, 0)),
        ],
        out_shape=[
            jax.ShapeDtypeStruct((nseq * nblk * ROWS, D_MODEL), BF16),
            jax.ShapeDtypeStruct((nseq * nblk * ROWS, SSD_INNER), BF16),
            jax.ShapeDtypeStruct((nseq, CONV_A_W - 1, D_MODEL), F32),
            jax.ShapeDtypeStruct((nseq, SSD_CONV_W - 1, SSD_CONV_DIM), F32),
            jax.ShapeDtypeStruct((nseq, SSD_INNER, SSD_STATE), F32),
        ],
        scratch_shapes=_mixer_scratch(),
        compiler_params=_cparams(("arbitrary", "arbitrary")),
        name="mixer_prompt",
    )(proj, proj, *weights)


def _mixer_sample(proj, ca0, cs0, ss0, weights, nseq, row0):
    dt_blk = COL_DT // LANES
    blk0 = row0 // 8
    return pl.pallas_call(
        _mixer_sample_kernel,
        grid=(nseq,),
        in_specs=[
            pl.BlockSpec((8, MAIN_W), lambda s: (blk0 + s, 0)),
            pl.BlockSpec((8, LANES), lambda s: (blk0 + s, dt_blk)),
            pl.BlockSpec((1, CONV_A_W - 1, D_MODEL), lambda s: (s, 0, 0)),
            pl.BlockSpec((1, SSD_CONV_W - 1, SSD_CONV_DIM), lambda s: (s, 0, 0)),
            pl.BlockSpec((1, SSD_INNER, SSD_STATE), lambda s: (s, 0, 0)),
        ] + _mixer_weight_specs(1),
        out_specs=[
            pl.BlockSpec((8, D_MODEL), lambda s: (s, 0)),
            pl.BlockSpec((8, SSD_INNER), lambda s: (s, 0)),
            pl.BlockSpec((1, CONV_A_W - 1, D_MODEL), lambda s: (s, 0, 0)),
            pl.BlockSpec((1, SSD_CONV_W - 1, SSD_CONV_DIM), lambda s: (s, 0, 0)),
            pl.BlockSpec((1, SSD_INNER, SSD_STATE), lambda s: (s, 0, 0)),
        ],
        out_shape=[
            jax.ShapeDtypeStruct((nseq * 8, D_MODEL), BF16),
            jax.ShapeDtypeStruct((nseq * 8, SSD_INNER), BF16),
            jax.ShapeDtypeStruct((nseq, CONV_A_W - 1, D_MODEL), F32),
            jax.ShapeDtypeStruct((nseq, SSD_CONV_W - 1, SSD_CONV_DIM), F32),
            jax.ShapeDtypeStruct((nseq, SSD_INNER, SSD_STATE), F32),
        ],
        scratch_shapes=_mixer_scratch() + [
            pltpu.VMEM((ROWS, MAIN_W), F32),
            pltpu.VMEM((8, SSD_CONV_DIM), F32),
        ],
        compiler_params=_cparams(("arbitrary",)),
        name="mixer_sample",
    )(proj, proj, ca0, cs0, ss0, *weights)


def _merge_kernel(n_p, x_ref, yap_ref, yas_ref, ynp_ref, yns_ref, g_ref, wa_ref, ws_ref, wo_ref,
                  nw_ref, x2_ref, hn_ref):
    is_sample = pl.program_id(0) >= n_p
    ya = jnp.where(is_sample, yas_ref[...], yap_ref[...])
    yn = jnp.where(is_sample, yns_ref[...], ynp_ref[...])
    br_a = jnp.dot(ya, wa_ref[...], preferred_element_type=F32)
    br_b = jnp.dot(yn, ws_ref[...], preferred_element_type=F32)
    g = g_ref[...]
    merged = jax.nn.sigmoid(g[:, :D_MODEL]) * br_a + jax.nn.sigmoid(g[:, D_MODEL:]) * br_b
    mix = jnp.dot(merged.astype(BF16), wo_ref[...], preferred_element_type=F32)
    x2 = x_ref[...] + mix
    x2_ref[...] = x2
    inv = lax.rsqrt(jnp.mean(x2 * x2, axis=-1, keepdims=True) + NORM_EPS)
    hn_ref[...] = ((x2 * inv) * nw_ref[...]).T.astype(BF16)


def _merge(x_all, ya_p, ya_s, yn_p, yn_s, proj, wa, ws, wo, nw, tt):
    t = x_all.shape[0]
    assert ya_p.shape[0] % tt == 0 and ya_s.shape[0] % tt == 0
    n_p = ya_p.shape[0] // tt
    p_map = lambda i: (jnp.minimum(i, n_p - 1), 0)
    s_map = lambda i: (jnp.maximum(i - n_p, 0), 0)
    gblk = COL_G // (2 * D_MODEL)
    zero = lambda i: (0, 0)
    return pl.pallas_call(
        functools.partial(_merge_kernel, n_p),
        grid=(t // tt,),
        in_specs=[
            pl.BlockSpec((tt, D_MODEL), lambda i: (i, 0)),
            pl.BlockSpec((tt, D_MODEL), p_map),
            pl.BlockSpec((tt, D_MODEL), s_map),
            pl.BlockSpec((tt, SSD_INNER), p_map),
            pl.BlockSpec((tt, SSD_INNER), s_map),
            pl.BlockSpec((tt, 2 * D_MODEL), lambda i: (i, gblk)),
            pl.BlockSpec((D_MODEL, D_MODEL), zero),
            pl.BlockSpec((SSD_INNER, D_MODEL), zero),
            pl.BlockSpec((D_MODEL, D_MODEL), zero),
            pl.BlockSpec((1, D_MODEL), zero),
        ],
        out_specs=[
            pl.BlockSpec((tt, D_MODEL), lambda i: (i, 0)),
            pl.BlockSpec((D_MODEL, tt), lambda i: (0, i)),
        ],
        out_shape=[
            jax.ShapeDtypeStruct((t, D_MODEL), F32),
            jax.ShapeDtypeStruct((D_MODEL, t), BF16),
        ],
        compiler_params=_cparams(("arbitrary",)),
        name="merge",
    )(x_all, ya_p, ya_s, yn_p, yn_s, proj, wa, ws, wo, nw)


def _oddeven_merge_sort_pairs(n):
    pairs = []
    p = 1
    while p < n:
        k = p
        while k >= 1:
            j = k % p
            while j <= n - 1 - k:
                for i in range(min(k, n - j - k)):
                    if (i + j) // (2 * p) == (i + j + k) // (2 * p):
                        pairs.append((i + j, i + j + k))
                j += 2 * k
            k //= 2
        p *= 2
    return pairs


_SORT16 = _oddeven_merge_sort_pairs(16)
_BITONIC16 = [(i, i + d) for d in (8, 4, 2, 1) for i in range(16) if (i & d) == 0]
_CAND_PAIRS = [(r1, r2) for r1 in range(PEER_TOPK) for r2 in range(PEER_TOPK)
               if (r1 + 1) * (r2 + 1) <= PEER_TOPK]


def _ce(v, i, j):
    hi = jnp.maximum(v[i], v[j])
    lo = jnp.minimum(v[i], v[j])
    v[i], v[j] = hi, lo


def _top16_desc(v):
    v = list(v)
    for i, j in _SORT16:
        _ce(v, i, j)
    for shift in (4, 2, 1):
        p = [pltpu.roll(x, shift, 0) for x in v]
        v = [jnp.maximum(v[r], p[15 - r]) for r in range(16)]
        for i, j in _BITONIC16:
            _ce(v, i, j)
    return v


def _route_kernel(ht_ref, wq_ref, keys_ref, s1_ref, e1_ref, s2_ref, e2_ref, tau_ref, zinv_ref, q_ref):
    tt = ht_ref.shape[1]
    q_ref[...] = jnp.dot(wq_ref[...], ht_ref[...], preferred_element_type=F32).astype(BF16)
    sub = lax.broadcasted_iota(jnp.int32, (SUBLANES, tt), 0)
    tops = [[None] * PEER_TOPK, [None] * PEER_TOPK]
    s_refs, e_refs = (s1_ref, s2_ref), (e1_ref, e2_ref)
    for h in range(PEER_HEADS):
        for i in range(2):
            r0 = (h * 2 + i) * PEER_HALF
            s = jnp.dot(keys_ref[h * 2 + i], q_ref[r0:r0 + PEER_HALF, :],
                        preferred_element_type=F32)
            v = [s[8 * j:8 * j + 8, :] for j in range(16)]
            top = _top16_desc(v)
            thr, best = top[PEER_TOPK - 1], top[0]
            for j in range(16):
                s_refs[i][h, j] = jnp.where(v[j] >= thr, v[j], NEG_INF)
                e_refs[i][h, j] = jnp.exp(v[j] - best)
            for r in range(PEER_TOPK):
                tops[i][r] = top[r] if h == 0 else jnp.where(sub == h, top[r], tops[i][r])
    cand = [tops[0][r1] + tops[1][r2] for r1, r2 in _CAND_PAIRS]
    picked = []
    for _ in range(PEER_TOPK):
        m = cand[0]
        for c in cand[1:]:
            m = jnp.maximum(m, c)
        picked.append(m)
        found = jnp.zeros(m.shape, jnp.bool_)
        nxt = []
        for c in cand:
            eq = c == m
            take = jnp.logical_and(eq, jnp.logical_not(found))
            found = jnp.logical_or(found, eq)
            nxt.append(jnp.where(take, NEG_INF, c))
        cand = nxt
    zsum = jnp.zeros_like(picked[0])
    for c in picked:
        zsum = zsum + jnp.exp(c - picked[0])
    tau_ref[...] = picked[PEER_TOPK - 1]
    zinv_ref[...] = 1.0 / zsum


def _route(hn_t, wq_t, keys, tt):
    t = hn_t.shape[1]
    tab_shape = (PEER_HEADS, N_KEYS // SUBLANES, SUBLANES, t)
    tab_spec = pl.BlockSpec((PEER_HEADS, N_KEYS // SUBLANES, SUBLANES, tt), lambda i: (0, 0, 0, i))
    return pl.pallas_call(
        _route_kernel,
        grid=(t // tt,),
        in_specs=[
            pl.BlockSpec((D_MODEL, tt), lambda i: (0, i)),
            pl.BlockSpec(wq_t.shape, lambda i: (0, 0)),
            pl.BlockSpec(keys.shape, lambda i: (0, 0, 0)),
        ],
        out_specs=[tab_spec] * 4 + [
            pl.BlockSpec((PEER_HEADS, tt), lambda i: (0, i)),
            pl.BlockSpec((PEER_HEADS, tt), lambda i: (0, i)),
        ],
        out_shape=[jax.ShapeDtypeStruct(tab_shape, F32)] * 4 + [
            jax.ShapeDtypeStruct((PEER_HEADS, t), F32),
            jax.ShapeDtypeStruct((PEER_HEADS, t), F32),
        ],
        scratch_shapes=[pltpu.VMEM((wq_t.shape[0], tt), BF16)],
        compiler_params=_cparams(("arbitrary",)),
        name="peer_route",
    )(hn_t, wq_t, keys)


I1_PER_STEP = 8
I1_GROUP = 4
EXPERTS_PER_STEP = I1_PER_STEP * N_KEYS


def _expert_kernel(ht_ref, u_ref, vt_ref, s1_ref, e1_ref, s2_ref, e2_ref, tau_ref, zinv_ref,
                   x2_ref, nfw_ref, y_ref, acc_ref, w_ref, act_ref, e1z_ref):
    c = pl.program_id(1)
    tt = ht_ref.shape[1]

    @pl.when(c == 0)
    def _():
        acc_ref[...] = jnp.zeros_like(acc_ref)

    for h in range(PEER_HEADS):
        e1z_ref[h] = e1_ref[h, 0] * (0.5 * zinv_ref[h:h + 1, :])

    for j in range(I1_PER_STEP):
        act_ref[j % 2] = jnp.dot(u_ref[j * N_KEYS:(j + 1) * N_KEYS, :], ht_ref[...],
                                 preferred_element_type=F32)
        for half in range(tt // LANES):
            ls = pl.ds(half * LANES, LANES)
            taub = [jnp.broadcast_to(tau_ref[h:h + 1, ls], (BF16_ROWS, LANES))
                    for h in range(PEER_HEADS)]
            s1b = [jnp.broadcast_to(s1_ref[h, 0, j:j + 1, ls], (BF16_ROWS, LANES))
                   for h in range(PEER_HEADS)]
            e1b = [jnp.broadcast_to(e1z_ref[h, j:j + 1, ls], (BF16_ROWS, LANES))
                   for h in range(PEER_HEADS)]
            for bb in range(N_KEYS // BF16_ROWS):
                r = bb * BF16_ROWS
                g = None
                for h in range(PEER_HEADS):
                    v = s1b[h] + s2_ref[h, r:r + BF16_ROWS, ls]
                    t = jnp.where(v >= taub[h], e1b[h] * e2_ref[h, r:r + BF16_ROWS, ls], 0.0)
                    g = t if g is None else g + t
                a = act_ref[j % 2, r:r + BF16_ROWS, ls]
                ge = a * (1.0 + lax.erf(a * math.sqrt(0.5)))
                w_ref[j * N_KEYS + r:j * N_KEYS + r + BF16_ROWS, ls] = (ge * g).astype(BF16)
        if j % I1_GROUP == I1_GROUP - 1:
            k0 = (j + 1 - I1_GROUP) * N_KEYS
            acc_ref[...] += jnp.dot(vt_ref[:, k0:k0 + I1_GROUP * N_KEYS],
                                    w_ref[k0:k0 + I1_GROUP * N_KEYS, :], preferred_element_type=F32)

    @pl.when(c == pl.num_programs(1) - 1)
    def _():
        x3 = x2_ref[...] + acc_ref[...].T
        inv = lax.rsqrt(jnp.mean(x3 * x3, axis=-1, keepdims=True) + NORM_EPS)
        y_ref[...] = (x3 * inv) * nfw_ref[...]


def _experts(hn_t, u_bf, v_t, s1, e1, s2, e2, tau, zinv, x2, nfw, tt):
    t = hn_t.shape[1]
    n_exp = u_bf.shape[0]
    nc = n_exp // EXPERTS_PER_STEP
    assert s1.shape == (PEER_HEADS, nc, I1_PER_STEP, t)
    s2 = s2.reshape(PEER_HEADS, N_KEYS, t)
    e2 = e2.reshape(PEER_HEADS, N_KEYS, t)
    return pl.pallas_call(
        _expert_kernel,
        grid=(t // tt, nc),
        in_specs=[
            pl.BlockSpec((D_MODEL, tt), lambda i, c: (0, i)),
            pl.BlockSpec((EXPERTS_PER_STEP, D_MODEL), lambda i, c: (c, 0)),
            pl.BlockSpec((D_MODEL, EXPERTS_PER_STEP), lambda i, c: (0, c)),
            pl.BlockSpec((PEER_HEADS, 1, I1_PER_STEP, tt), lambda i, c: (0, c, 0, i)),
            pl.BlockSpec((PEER_HEADS, 1, I1_PER_STEP, tt), lambda i, c: (0, c, 0, i)),
            pl.BlockSpec((PEER_HEADS, N_KEYS, tt), lambda i, c: (0, 0, i)),
            pl.BlockSpec((PEER_HEADS, N_KEYS, tt), lambda i, c: (0, 0, i)),
            pl.BlockSpec((PEER_HEADS, tt), lambda i, c: (0, i)),
            pl.BlockSpec((PEER_HEADS, tt), lambda i, c: (0, i)),
            pl.BlockSpec((tt, D_MODEL), lambda i, c: (i, 0)),
            pl.BlockSpec((1, D_MODEL), lambda i, c: (0, 0)),
        ],
        out_specs=pl.BlockSpec((tt, D_MODEL), lambda i, c: (i, 0)),
        out_shape=jax.ShapeDtypeStruct((t, D_MODEL), F32),
        scratch_shapes=[
            pltpu.VMEM((D_MODEL, tt), F32),
            pltpu.VMEM((EXPERTS_PER_STEP, tt), BF16),
            pltpu.VMEM((2, N_KEYS, tt), F32),
            pltpu.VMEM((PEER_HEADS, I1_PER_STEP, tt), F32),
        ],
        compiler_params=_cparams(("arbitrary", "arbitrary")),
        name="peer_experts",
    )(hn_t, u_bf, v_t, s1, e1, s2, e2, tau, zinv, x2, nfw)


def _pick_tile(t, prefs):
    for p in prefs:
        if t % p == 0:
            return p
    raise ValueError(f"token count {t} not tileable by {prefs}")


def _layer(x_all, nseq_p, nblk, ca0, cs0, ss0, lw, norm_final_w):
    (norm_mix_w, w_in, conv_a_w, w_a_out, ssd_conv_w, ssd_conv_b, ssd_dt_bias, ssd_a_log, ssd_d,
     ssd_norm_w, w_ssd_out, w_o, norm_ffn_w, peer_w_q, peer_sub_keys, peer_u, peer_v) = lw
    t_all = x_all.shape[0]
    t_p = nseq_p * nblk * ROWS
    nseq_s = ca0.shape[0]

    c = np.cumsum([0, 1024, 1024, 1024, SSD_INNER, SSD_CONV_DIM, SSD_HEADS, 1024, 1024])
    w_in_r = jnp.concatenate(
        [w_in[:, c[0]:c[5]], w_in[:, c[6]:c[8]], w_in[:, c[5]:c[6]],
         jnp.zeros((D_MODEL, LANES - SSD_HEADS), w_in.dtype)], axis=1).astype(BF16)
    pad_h = lambda v: jnp.pad(v.astype(F32), (0, LANES - SSD_HEADS)).reshape(1, LANES)
    mixer_w = (conv_a_w.astype(F32), ssd_conv_w.astype(F32), ssd_conv_b.reshape(1, -1).astype(F32),
               pad_h(ssd_dt_bias), pad_h(ssd_a_log),
               jnp.repeat(ssd_d.astype(F32), SSD_HEADDIM).reshape(1, SSD_INNER),
               ssd_norm_w.reshape(1, SSD_INNER).astype(F32))

    proj = _inproj(x_all, norm_mix_w.reshape(1, D_MODEL), w_in_r,
                   _pick_tile(t_all, (1024, 512, 256)), 1152)
    ya_p, yn_p, nca_p, ncs_p, nss_p = _mixer_prompt(proj, mixer_w, nseq_p, nblk)
    ya_s, yn_s, nca_s, ncs_s, nss_s = _mixer_sample(
        proj, ca0, cs0, ss0.reshape(nseq_s, SSD_INNER, SSD_STATE), mixer_w, nseq_s, t_p)

    x2, hn_t = _merge(x_all, ya_p, ya_s, yn_p, yn_s, proj, w_a_out.astype(BF16), w_ssd_out.astype(BF16),
                      w_o.astype(BF16), norm_ffn_w.reshape(1, D_MODEL),
                      _pick_tile(t_all, (512, 256)))

    wq_t = peer_w_q.T.astype(BF16)
    keys = peer_sub_keys.reshape(PEER_HEADS * 2, N_KEYS, PEER_HALF).astype(BF16)
    s1, e1, s2, e2, tau, zinv = _route(hn_t, wq_t, keys, _pick_tile(t_all, (256, 128)))
    y = _experts(hn_t, peer_u.astype(BF16), peer_v.T.astype(BF16), s1, e1, s2, e2, tau, zinv, x2,
                 norm_final_w.reshape(1, D_MODEL), _pick_tile(t_all, (256, 128)))
    shp = (SSD_HEADS, SSD_HEADDIM, SSD_STATE)
    return (y, nca_p, ncs_p, nss_p.reshape(nseq_p, *shp), nca_s, ncs_s, nss_s.reshape(nseq_s, *shp))


def kernel(x_prompt, x_sample, state_conv_a, state_conv_ssd, state_ssm, meta_tokens, norm_mix_w, w_in, conv_a_w, w_a_out, ssd_conv_w, ssd_conv_b, ssd_dt_bias, ssd_a_log, ssd_d, ssd_norm_w, w_ssd_out, w_o, norm_ffn_w, peer_w_q, peer_sub_keys, peer_u, peer_v, norm_final_w):
    bp, s_len, d = x_prompt.shape
    bs, ds_len, _ = x_sample.shape
    depth = w_in.shape[0]
    assert depth == 1 and d == D_MODEL and ds_len == 8 and s_len % ROWS == 0
    nblk = s_len // ROWS + 1
    head = jnp.concatenate(
        [jnp.zeros((ROWS - N_META, d), x_prompt.dtype), meta_tokens.astype(x_prompt.dtype)], axis=0)
    xp = jnp.concatenate([jnp.broadcast_to(head[None], (bp, ROWS, d)), x_prompt], axis=1)
    x_all = jnp.concatenate([xp.reshape(-1, d), x_sample.reshape(-1, d)], axis=0)
    sq = lambda a: a.reshape(a.shape[1:])
    lw = tuple(sq(a) for a in (
        norm_mix_w, w_in, conv_a_w, w_a_out, ssd_conv_w, ssd_conv_b, ssd_dt_bias, ssd_a_log,
        ssd_d, ssd_norm_w, w_ssd_out, w_o, norm_ffn_w, peer_w_q, peer_sub_keys, peer_u, peer_v))
    y, nca_p, ncs_p, nss_p, nca_s, ncs_s, nss_s = _layer(
        x_all, bp, nblk, sq(state_conv_a), sq(state_conv_ssd), sq(state_ssm), lw, norm_final_w)
    t_p = bp * nblk * ROWS
    y_prompt = y[:t_p].reshape(bp, nblk * ROWS, d)[:, ROWS:]
    y_sample = y[t_p:].reshape(bs, ds_len, d)
    return (y_prompt, y_sample, nca_p[None], ncs_p[None], nss_p[None],
            nca_s[None], ncs_s[None], nss_s[None])
```

```python
import functools
import math

import numpy as np
import jax
import jax.numpy as jnp
from jax import lax
from jax.experimental import pallas as pl
from jax.experimental.pallas import tpu as pltpu

F32 = jnp.float32
BF16 = jnp.bfloat16

D_MODEL = 1024
N_META = 16
NORM_EPS = 1e-6
CONV_A_W = 3
SSD_INNER = 2048
SSD_HEADDIM = 64
SSD_HEADS = 32
SSD_STATE = 128
SSD_GROUPS = 4
SSD_HPG = 8
SSD_CONV_W = 4
SSD_CONV_DIM = SSD_INNER + 2 * SSD_GROUPS * SSD_STATE
PEER_HEADS = 8
N_KEYS = 128
PEER_TOPK = 16
PEER_HALF = 128

ROWS = 128
LANES = 128
SUBLANES = 8
BF16_ROWS = 16
COL_AB, COL_AC, COL_AX, COL_Z, COL_XBC = 0, 1024, 2048, 3072, 5120
COL_G = 8192
COL_DT = 10240
MXU_COLS = 256
INPROJ_NT = 7 * MXU_COLS
IN_DIM_R = 6 * INPROJ_NT
assert IN_DIM_R >= COL_DT + LANES
MAIN_W = COL_G
NEG_INF = float("-inf")

VMEM_LIMIT = 56 * 1024 * 1024


def _cparams(sem, flags=None):
    return pltpu.CompilerParams(dimension_semantics=sem, vmem_limit_bytes=VMEM_LIMIT, flags=flags)


def _inproj_kernel(x_ref, nw_ref, w_ref, o_ref, hn_ref):
    @pl.when(pl.program_id(1) == 0)
    def _():
        x = x_ref[...]
        inv = lax.rsqrt(jnp.mean(x * x, axis=-1, keepdims=True) + NORM_EPS)
        hn_ref[...] = ((x * inv) * nw_ref[...]).astype(BF16)

    o_ref[...] = jnp.dot(hn_ref[...], w_ref[...], preferred_element_type=F32)


def _inproj(x_all, norm_w, w_in_r, tt, nt):
    t = x_all.shape[0]
    n = w_in_r.shape[1]
    return pl.pallas_call(
        _inproj_kernel,
        grid=(t // tt, n // nt),
        in_specs=[
            pl.BlockSpec((tt, D_MODEL), lambda i, j: (i, 0)),
            pl.BlockSpec((1, D_MODEL), lambda i, j: (0, 0)),
            pl.BlockSpec((D_MODEL, nt), lambda i, j: (0, j)),
        ],
        out_specs=pl.BlockSpec((tt, nt), lambda i, j: (i, j)),
        out_shape=jax.ShapeDtypeStruct((t, n), F32),
        scratch_shapes=[pltpu.VMEM((tt, D_MODEL), BF16)],
        compiler_params=_cparams(("arbitrary", "arbitrary")),
        name="inproj",
    )(x_all, norm_w, w_in_r)


def _silu(v):
    return v * jax.nn.sigmoid(v)


def _mixer_block(a_b, u, z, xbc, dt_raw, r0, su_ref, sx_ref, s_ref, gated_ref,
                 caw_ref, scw_ref, scb_ref, dtb_ref, alog_ref, dexp_ref, nrm_ref):
    R = ROWS
    su_ref[8:8 + R, :] = u
    u1 = su_ref[7:7 + R, :]
    u2 = su_ref[6:6 + R, :]
    conv_a = caw_ref[2:3, :] * u + caw_ref[1:2, :] * u1 + caw_ref[0:1, :] * u2
    ya = (a_b * conv_a).astype(BF16)
    su_ref[0:8, :] = su_ref[R:R + 8, :]

    sx_ref[8:8 + R, :] = xbc
    x1 = sx_ref[7:7 + R, :]
    x2 = sx_ref[6:6 + R, :]
    x3 = sx_ref[5:5 + R, :]
    xc = (scw_ref[3:4, :] * xbc + scw_ref[2:3, :] * x1 + scw_ref[1:2, :] * x2
          + scw_ref[0:1, :] * x3 + scb_ref[...])
    xc = _silu(xc)
    sx_ref[0:8, :] = sx_ref[R:R + 8, :]

    row = lax.broadcasted_iota(jnp.int32, (R, LANES), 0)
    col = lax.broadcasted_iota(jnp.int32, (R, LANES), 1)
    valid = row >= r0
    dt = jnp.where(valid, jax.nn.softplus(dt_raw + dtb_ref[...]), 0.0)
    a = -jnp.exp(alog_ref[...])
    da = dt * a
    tri = row >= col
    tri_f = tri.astype(F32)
    cum = jnp.dot(tri_f, da, preferred_element_type=F32, precision=lax.Precision.HIGHEST)
    da_t = da.T
    dt_t = dt.T
    triu_f = (row <= col).astype(F32)
    cum_t = jnp.dot(da_t, triu_f, preferred_element_type=F32, precision=lax.Precision.HIGHEST)
    ecum = jnp.exp(cum)
    cum_last = cum[R - 1:R, :]
    wend = jnp.exp(cum_last - cum) * dt
    dec_h = jnp.broadcast_to(jnp.exp(cum_t[:, R - 1:R]), (LANES, LANES))

    lane_lo = col < SSD_HEADDIM
    for g in range(SSD_GROUPS):
        b_g = xc[:, SSD_INNER + g * SSD_STATE: SSD_INNER + (g + 1) * SSD_STATE]
        c_g = xc[:, SSD_INNER + SSD_GROUPS * SSD_STATE + g * SSD_STATE:
                 SSD_INNER + SSD_GROUPS * SSD_STATE + (g + 1) * SSD_STATE]
        b_bf = b_g.astype(BF16)
        c_bf = c_g.astype(BF16)
        cb = lax.dot_general(c_bf, b_bf, (((1,), (1,)), ((), ())), preferred_element_type=F32)
        s_g = s_ref[g * 512:(g + 1) * 512, :]
        yo_g = lax.dot_general(c_bf, s_g.astype(BF16), (((1,), (1,)), ((), ())),
                               preferred_element_type=F32)
        for pr in range(SSD_HPG // 2):
            h0 = g * SSD_HPG + 2 * pr
            h1 = h0 + 1
            c0 = (h0 * SSD_HEADDIM)
            x_pair = xc[:, c0:c0 + LANES]
            lmats = []
            for h in (h0, h1):
                seg = jnp.where(tri, cum[:, h:h + 1] - cum_t[h:h + 1, :], NEG_INF)
                lmats.append((cb * jnp.exp(seg) * dt_t[h:h + 1, :]).astype(BF16))
            x_lo = jnp.where(lane_lo, x_pair, 0.0).astype(BF16)
            x_hi = jnp.where(lane_lo, 0.0, x_pair).astype(BF16)
            yd = (jnp.dot(lmats[0], x_lo, preferred_element_type=F32)
                  + jnp.dot(lmats[1], x_hi, preferred_element_type=F32))
            scal = jnp.where(lane_lo, ecum[:, h0:h0 + 1], ecum[:, h1:h1 + 1])
            yo = yo_g[:, 2 * pr * SSD_HEADDIM: 2 * pr * SSD_HEADDIM + LANES]
            y = yd + yo * scal + dexp_ref[:, c0:c0 + LANES] * x_pair
            zz = z[:, c0:c0 + LANES]
            gated_ref[:, c0:c0 + LANES] = y * _silu(zz)
            wsel = jnp.where(lane_lo, wend[:, h0:h0 + 1], wend[:, h1:h1 + 1])
            xw_t = (x_pair * wsel).T.astype(BF16)
            cs = jnp.dot(xw_t, b_bf, preferred_element_type=F32)
            dec = jnp.concatenate(
                [jnp.broadcast_to(dec_h[h0:h0 + 1, :], (SSD_HEADDIM, LANES)),
                 jnp.broadcast_to(dec_h[h1:h1 + 1, :], (SSD_HEADDIM, LANES))], axis=0)
            r_lo = h0 * SSD_HEADDIM
            s_ref[r_lo:r_lo + LANES, :] = dec * s_ref[r_lo:r_lo + LANES, :] + cs

    outs = []
    for g in range(SSD_GROUPS):
        gg = gated_ref[:, g * 512:(g + 1) * 512]
        ms = jnp.mean(gg * gg, axis=-1, keepdims=True)
        outs.append(((gg * lax.rsqrt(ms + NORM_EPS)).astype(F32)
                     * nrm_ref[:, g * 512:(g + 1) * 512]).astype(BF16))
    yn = jnp.concatenate(outs, axis=1)
    return ya, yn


def _mixer_prompt_kernel(main_ref, dt_ref, caw_ref, scw_ref, scb_ref, dtb_ref, alog_ref,
                         dexp_ref, nrm_ref,
                         ya_ref, yn_ref, nca_ref, ncs_ref, nss_ref,
                         su_ref, sx_ref, s_ref, gated_ref):
    j = pl.program_id(1)

    @pl.when(j == 0)
    def _():
        su_ref[0:8, :] = jnp.zeros((8, D_MODEL), F32)
        sx_ref[0:8, :] = jnp.zeros((8, SSD_CONV_DIM), F32)
        s_ref[...] = jnp.zeros_like(s_ref)

    a_b = main_ref[:, COL_AB:COL_AB + 1024]
    u = main_ref[:, COL_AC:COL_AC + 1024] * main_ref[:, COL_AX:COL_AX + 1024]
    z = main_ref[:, COL_Z:COL_Z + 2048]
    xbc = main_ref[:, COL_XBC:COL_XBC + SSD_CONV_DIM]
    r0 = jnp.where(j == 0, ROWS - N_META, 0)
    ya, yn = _mixer_block(a_b, u, z, xbc, dt_ref[...], r0, su_ref, sx_ref, s_ref, gated_ref,
                          caw_ref, scw_ref, scb_ref, dtb_ref, alog_ref, dexp_ref, nrm_ref)
    ya_ref[...] = ya
    yn_ref[...] = yn
    nca_ref[0] = su_ref[6:8, :]
    ncs_ref[0] = sx_ref[5:8, :]
    nss_ref[0] = s_ref[...]


def _mixer_sample_kernel(main_ref, dt_ref, ca0_ref, cs0_ref, ss0_ref,
                         caw_ref, scw_ref, scb_ref, dtb_ref, alog_ref, dexp_ref, nrm_ref,
                         ya_ref, yn_ref, nca_ref, ncs_ref, nss_ref,
                         su_ref, sx_ref, s_ref, gated_ref, pad_ref, tmp_ref):
    R = ROWS
    n = 8
    su_ref[0:8, :] = jnp.zeros((8, D_MODEL), F32)
    sx_ref[0:8, :] = jnp.zeros((8, SSD_CONV_DIM), F32)
    s_ref[...] = ss0_ref[0]

    pad_ref[...] = jnp.zeros_like(pad_ref)
    pad_ref[R - n:R, :] = main_ref[...]
    a_b = pad_ref[:, COL_AB:COL_AB + 1024]
    z = pad_ref[:, COL_Z:COL_Z + 2048]
    tmp_ref[...] = jnp.zeros_like(tmp_ref)
    tmp_ref[6:8, 0:D_MODEL] = ca0_ref[0]
    u = jnp.concatenate(
        [jnp.zeros((R - 2 * n, D_MODEL), F32), tmp_ref[:, 0:D_MODEL],
         main_ref[:, COL_AC:COL_AC + 1024] * main_ref[:, COL_AX:COL_AX + 1024]], axis=0)
    tmp_ref[...] = jnp.zeros_like(tmp_ref)
    tmp_ref[5:8, :] = cs0_ref[0]
    xbc = jnp.concatenate(
        [jnp.zeros((R - 2 * n, SSD_CONV_DIM), F32), tmp_ref[...],
         main_ref[:, COL_XBC:COL_XBC + SSD_CONV_DIM]], axis=0)
    dt_raw = jnp.concatenate([jnp.zeros((R - n, LANES), F32), dt_ref[...]], axis=0)
    ya, yn = _mixer_block(a_b, u, z, xbc, dt_raw, R - n, su_ref, sx_ref, s_ref, gated_ref,
                          caw_ref, scw_ref, scb_ref, dtb_ref, alog_ref, dexp_ref, nrm_ref)
    ya_ref[...] = ya[R - n:R, :]
    yn_ref[...] = yn[R - n:R, :]
    nca_ref[0] = su_ref[6:8, :]
    ncs_ref[0] = sx_ref[5:8, :]
    nss_ref[0] = s_ref[...]


def _mixer_weight_specs(nidx):
    zero = (lambda *a: (0, 0))
    del nidx
    return [
        pl.BlockSpec((CONV_A_W, D_MODEL), zero),
        pl.BlockSpec((SSD_CONV_W, SSD_CONV_DIM), zero),
        pl.BlockSpec((1, SSD_CONV_DIM), zero),
        pl.BlockSpec((1, LANES), zero),
        pl.BlockSpec((1, LANES), zero),
        pl.BlockSpec((1, SSD_INNER), zero),
        pl.BlockSpec((1, SSD_INNER), zero),
    ]


def _mixer_scratch():
    return [
        pltpu.VMEM((ROWS + 8, D_MODEL), F32),
        pltpu.VMEM((ROWS + 8, SSD_CONV_DIM), F32),
        pltpu.VMEM((SSD_INNER, SSD_STATE), F32),
        pltpu.VMEM((ROWS, SSD_INNER), F32),
    ]


def _mixer_prompt(proj, weights, nseq, nblk, meta_row0):
    dt_blk = COL_DT // LANES
    meta_blk0 = meta_row0 // ROWS
    rblk = lambda b, j: jnp.where(j == 0, meta_blk0 + b, b * (nblk - 1) + j - 1)
    return pl.pallas_call(
        _mixer_prompt_kernel,
        grid=(nseq, nblk),
        in_specs=[
            pl.BlockSpec((ROWS, MAIN_W), lambda b, j: (rblk(b, j), 0)),
            pl.BlockSpec((ROWS, LANES), lambda b, j: (rblk(b, j), dt_blk)),
        ] + _mixer_weight_specs(2),
        out_specs=[
            pl.BlockSpec((ROWS, D_MODEL), lambda b, j: (rblk(b, j), 0)),
            pl.BlockSpec((ROWS, SSD_INNER), lambda b, j: (rblk(b, j), 0)),
            pl.BlockSpec((1, CONV_A_W - 1, D_MODEL), lambda b, j: (b, 0, 0)),
            pl.BlockSpec((1, SSD_CONV_W - 1, SSD_CONV_DIM), lambda b, j: (b, 0, 0)),
            pl.BlockSpec((1, SSD_INNER, SSD_STATE), lambda b, j: (b, 0, 0)),
        ],
        out_shape=[
            jax.ShapeDtypeStruct((proj.shape[0], D_MODEL), BF16),
            jax.ShapeDtypeStruct((proj.shape[0], SSD_INNER), BF16),
            jax.ShapeDtypeStruct((nseq, CONV_A_W - 1, D_MODEL), F32),
            jax.ShapeDtypeStruct((nseq, SSD_CONV_W - 1, SSD_CONV_DIM), F32),
            jax.ShapeDtypeStruct((nseq, SSD_INNER, SSD_STATE), F32),
        ],
        scratch_shapes=_mixer_scratch(),
        compiler_params=_cparams(("arbitrary", "arbitrary")),
        name="mixer_prompt",
    )(proj, proj, *weights)


def _mixer_sample(proj, ca0, cs0, ss0, weights, nseq, row0):
    dt_blk = COL_DT // LANES
    blk0 = row0 // 8
    return pl.pallas_call(
        _mixer_sample_kernel,
        grid=(nseq,),
        in_specs=[
            pl.BlockSpec((8, MAIN_W), lambda s: (blk0 + s, 0)),
            pl.BlockSpec((8, LANES), lambda s: (blk0 + s, dt_blk)),
            pl.BlockSpec((1, CONV_A_W - 1, D_MODEL), lambda s: (s, 0, 0)),
            pl.BlockSpec((1, SSD_CONV_W - 1, SSD_CONV_DIM), lambda s: (s, 0, 0)),
            pl.BlockSpec((1, SSD_INNER, SSD_STATE), lambda s: (s, 0, 0)),
        ] + _mixer_weight_specs(1),
        out_specs=[
            pl.BlockSpec((8, D_MODEL), lambda s: (s, 0)),
            pl.BlockSpec((8, SSD_INNER), lambda s: (s, 0)),
            pl.BlockSpec((1, CONV_A_W - 1, D_MODEL), lambda s: (s, 0, 0)),
            pl.BlockSpec((1, SSD_CONV_W - 1, SSD_CONV_DIM), lambda s: (s, 0, 0)),
            pl.BlockSpec((1, SSD_INNER, SSD_STATE), lambda s: (s, 0, 0)),
        ],
        out_shape=[
            jax.ShapeDtypeStruct((nseq * 8, D_MODEL), BF16),
            jax.ShapeDtypeStruct((nseq * 8, SSD_INNER), BF16),
            jax.ShapeDtypeStruct((nseq, CONV_A_W - 1, D_MODEL), F32),
            jax.ShapeDtypeStruct((nseq, SSD_CONV_W - 1, SSD_CONV_DIM), F32),
            jax.ShapeDtypeStruct((nseq, SSD_INNER, SSD_STATE), F32),
        ],
        scratch_shapes=_mixer_scratch() + [
            pltpu.VMEM((ROWS, MAIN_W), F32),
            pltpu.VMEM((8, SSD_CONV_DIM), F32),
        ],
        compiler_params=_cparams(("arbitrary",)),
        name="mixer_sample",
    )(proj, proj, ca0, cs0, ss0, *weights)


def _merge_kernel(s_lo, s_hi, x_ref, yap_ref, yas_ref, ynp_ref, yns_ref, g_ref, wa_ref, ws_ref,
                  wo_ref, nw_ref, x2_ref, hn_ref):
    i = pl.program_id(0)
    is_sample = jnp.logical_and(i >= s_lo, i < s_hi)
    ya = jnp.where(is_sample, yas_ref[...], yap_ref[...])
    yn = jnp.where(is_sample, yns_ref[...], ynp_ref[...])
    br_a = jnp.dot(ya, wa_ref[...], preferred_element_type=F32)
    br_b = jnp.dot(yn, ws_ref[...], preferred_element_type=F32)
    g = g_ref[...]
    merged = jax.nn.sigmoid(g[:, :D_MODEL]) * br_a + jax.nn.sigmoid(g[:, D_MODEL:]) * br_b
    mix = jnp.dot(merged.astype(BF16), wo_ref[...], preferred_element_type=F32)
    x2 = x_ref[...] + mix
    x2_ref[...] = x2
    inv = lax.rsqrt(jnp.mean(x2 * x2, axis=-1, keepdims=True) + NORM_EPS)
    hn_ref[...] = ((x2 * inv) * nw_ref[...]).T.astype(BF16)


def _merge(x_all, ya_p, ya_s, yn_p, yn_s, proj, wa, ws, wo, nw, tt, sample_row0):
    t = x_all.shape[0]
    assert sample_row0 % tt == 0 and ya_s.shape[0] % tt == 0 and ya_p.shape[0] == t
    s_lo = sample_row0 // tt
    s_hi = s_lo + ya_s.shape[0] // tt
    in_s = lambda i: jnp.logical_and(i >= s_lo, i < s_hi)
    p_map = lambda i: (jnp.where(in_s(i), 0, i), 0)
    s_map = lambda i: (jnp.clip(i - s_lo, 0, s_hi - s_lo - 1), 0)
    gblk = COL_G // (2 * D_MODEL)
    zero = lambda i: (0, 0)
    return pl.pallas_call(
        functools.partial(_merge_kernel, s_lo, s_hi),
        grid=(t // tt,),
        in_specs=[
            pl.BlockSpec((tt, D_MODEL), lambda i: (i, 0)),
            pl.BlockSpec((tt, D_MODEL), p_map),
            pl.BlockSpec((tt, D_MODEL), s_map),
            pl.BlockSpec((tt, SSD_INNER), p_map),
            pl.BlockSpec((tt, SSD_INNER), s_map),
            pl.BlockSpec((tt, 2 * D_MODEL), lambda i: (i, gblk)),
            pl.BlockSpec((D_MODEL, D_MODEL), zero),
            pl.BlockSpec((SSD_INNER, D_MODEL), zero),
            pl.BlockSpec((D_MODEL, D_MODEL), zero),
            pl.BlockSpec((1, D_MODEL), zero),
        ],
        out_specs=[
            pl.BlockSpec((tt, D_MODEL), lambda i: (i, 0)),
            pl.BlockSpec((D_MODEL, tt), lambda i: (0, i)),
        ],
        out_shape=[
            jax.ShapeDtypeStruct((t, D_MODEL), F32),
            jax.ShapeDtypeStruct((D_MODEL, t), BF16),
        ],
        compiler_params=_cparams(("arbitrary",)),
        name="merge",
    )(x_all, ya_p, ya_s, yn_p, yn_s, proj, wa, ws, wo, nw)


def _oddeven_merge_sort_pairs(n):
    pairs = []
    p = 1
    while p < n:
        k = p
        while k >= 1:
            j = k % p
            while j <= n - 1 - k:
                for i in range(min(k, n - j - k)):
                    if (i + j) // (2 * p) == (i + j + k) // (2 * p):
                        pairs.append((i + j, i + j + k))
                j += 2 * k
            k //= 2
        p *= 2
    return pairs


_SORT16 = _oddeven_merge_sort_pairs(16)
_BITONIC16 = [(i, i + d) for d in (8, 4, 2, 1) for i in range(16) if (i & d) == 0]
_CAND_PAIRS = [(r1, r2) for r1 in range(PEER_TOPK) for r2 in range(PEER_TOPK)
               if (r1 + 1) * (r2 + 1) <= PEER_TOPK]


def _ce(v, i, j):
    hi = jnp.maximum(v[i], v[j])
    lo = jnp.minimum(v[i], v[j])
    v[i], v[j] = hi, lo


def _top16_desc(v):
    v = list(v)
    for i, j in _SORT16:
        _ce(v, i, j)
    for shift in (4, 2, 1):
        p = [pltpu.roll(x, shift, 0) for x in v]
        v = [jnp.maximum(v[r], p[15 - r]) for r in range(16)]
        for i, j in _BITONIC16:
            _ce(v, i, j)
    return v


def _route_kernel(ht_ref, wq_ref, keys_ref, s1_ref, e1_ref, s2_ref, e2_ref, tau_ref, zinv_ref, q_ref):
    tt = ht_ref.shape[1]
    q_ref[...] = jnp.dot(wq_ref[...], ht_ref[...], preferred_element_type=F32).astype(BF16)
    sub = lax.broadcasted_iota(jnp.int32, (SUBLANES, tt), 0)
    tops = [[None] * PEER_TOPK, [None] * PEER_TOPK]
    s_refs, e_refs = (s1_ref, s2_ref), (e1_ref, e2_ref)
    for h in range(PEER_HEADS):
        for i in range(2):
            r0 = (h * 2 + i) * PEER_HALF
            s = jnp.dot(keys_ref[h * 2 + i], q_ref[r0:r0 + PEER_HALF, :],
                        preferred_element_type=F32)
            v = [s[8 * j:8 * j + 8, :] for j in range(16)]
            top = _top16_desc(v)
            thr, best = top[PEER_TOPK - 1], top[0]
            for j in range(16):
                s_refs[i][h, j] = jnp.where(v[j] >= thr, v[j], NEG_INF)
                e_refs[i][h, j] = jnp.exp(v[j] - best)
            for r in range(PEER_TOPK):
                tops[i][r] = top[r] if h == 0 else jnp.where(sub == h, top[r], tops[i][r])
    cand = [tops[0][r1] + tops[1][r2] for r1, r2 in _CAND_PAIRS]
    picked = []
    for _ in range(PEER_TOPK):
        m = cand[0]
        for c in cand[1:]:
            m = jnp.maximum(m, c)
        picked.append(m)
        found = jnp.zeros(m.shape, jnp.bool_)
        nxt = []
        for c in cand:
            eq = c == m
            take = jnp.logical_and(eq, jnp.logical_not(found))
            found = jnp.logical_or(found, eq)
            nxt.append(jnp.where(take, NEG_INF, c))
        cand = nxt
    zsum = jnp.zeros_like(picked[0])
    for c in picked:
        zsum = zsum + jnp.exp(c - picked[0])
    tau_ref[...] = picked[PEER_TOPK - 1]
    zinv_ref[...] = 1.0 / zsum


def _route(hn_t, wq_t, keys, tt):
    t = hn_t.shape[1]
    tab_shape = (PEER_HEADS, N_KEYS // SUBLANES, SUBLANES, t)
    tab_spec = pl.BlockSpec((PEER_HEADS, N_KEYS // SUBLANES, SUBLANES, tt), lambda i: (0, 0, 0, i))
    return pl.pallas_call(
        _route_kernel,
        grid=(t // tt,),
        in_specs=[
            pl.BlockSpec((D_MODEL, tt), lambda i: (0, i)),
            pl.BlockSpec(wq_t.shape, lambda i: (0, 0)),
            pl.BlockSpec(keys.shape, lambda i: (0, 0, 0)),
        ],
        out_specs=[tab_spec] * 4 + [
            pl.BlockSpec((PEER_HEADS, tt), lambda i: (0, i)),
            pl.BlockSpec((PEER_HEADS, tt), lambda i: (0, i)),
        ],
        out_shape=[jax.ShapeDtypeStruct(tab_shape, F32)] * 4 + [
            jax.ShapeDtypeStruct((PEER_HEADS, t), F32),
            jax.ShapeDtypeStruct((PEER_HEADS, t), F32),
        ],
        scratch_shapes=[pltpu.VMEM((wq_t.shape[0], tt), BF16)],
        compiler_params=_cparams(("arbitrary",)),
        name="peer_route",
    )(hn_t, wq_t, keys)


I1_PER_STEP = 8
I1_GROUP = 4
EXPERTS_PER_STEP = I1_PER_STEP * N_KEYS


def _expert_kernel(n_first, ht_ref, u_ref, vt_ref, s1_ref, e1_ref, s2_ref, e2_ref, tau_ref,
                   zinv_ref, x2_ref, nfw_ref, ya_ref, yb_ref, acc_ref, w_ref, act_ref, e1z_ref):
    c = pl.program_id(1)
    tt = ht_ref.shape[1]

    @pl.when(c == 0)
    def _():
        acc_ref[...] = jnp.zeros_like(acc_ref)

    for h in range(PEER_HEADS):
        e1z_ref[h] = e1_ref[h, 0] * (0.5 * zinv_ref[h:h + 1, :])

    for j in range(I1_PER_STEP):
        act_ref[j % 2] = jnp.dot(u_ref[j * N_KEYS:(j + 1) * N_KEYS, :], ht_ref[...],
                                 preferred_element_type=F32)
        for half in range(tt // LANES):
            ls = pl.ds(half * LANES, LANES)
            taub = [jnp.broadcast_to(tau_ref[h:h + 1, ls], (BF16_ROWS, LANES))
                    for h in range(PEER_HEADS)]
            s1b = [jnp.broadcast_to(s1_ref[h, 0, j:j + 1, ls], (BF16_ROWS, LANES))
                   for h in range(PEER_HEADS)]
            e1b = [jnp.broadcast_to(e1z_ref[h, j:j + 1, ls], (BF16_ROWS, LANES))
                   for h in range(PEER_HEADS)]
            for bb in range(N_KEYS // BF16_ROWS):
                r = bb * BF16_ROWS
                g = None
                for h in range(PEER_HEADS):
                    v = s1b[h] + s2_ref[h, r:r + BF16_ROWS, ls]
                    t = jnp.where(v >= taub[h], e1b[h] * e2_ref[h, r:r + BF16_ROWS, ls], 0.0)
                    g = t if g is None else g + t
                a = act_ref[j % 2, r:r + BF16_ROWS, ls]
                ge = a * (1.0 + lax.erf(a * math.sqrt(0.5)))
                w_ref[j * N_KEYS + r:j * N_KEYS + r + BF16_ROWS, ls] = (ge * g).astype(BF16)
        if j % I1_GROUP == I1_GROUP - 1:
            k0 = (j + 1 - I1_GROUP) * N_KEYS
            acc_ref[...] += jnp.dot(vt_ref[:, k0:k0 + I1_GROUP * N_KEYS],
                                    w_ref[k0:k0 + I1_GROUP * N_KEYS, :], preferred_element_type=F32)

    @pl.when(c == pl.num_programs(1) - 1)
    def _():
        x3 = x2_ref[...] + acc_ref[...].T
        inv = lax.rsqrt(jnp.mean(x3 * x3, axis=-1, keepdims=True) + NORM_EPS)
        y = (x3 * inv) * nfw_ref[...]
        first = pl.program_id(0) < n_first

        @pl.when(first)
        def _():
            ya_ref[...] = y

        @pl.when(jnp.logical_not(first))
        def _():
            yb_ref[...] = y


def _experts(hn_t, u_bf, v_t, s1, e1, s2, e2, tau, zinv, x2, nfw, tt, rows_a):
    t = hn_t.shape[1]
    n_exp = u_bf.shape[0]
    nc = n_exp // EXPERTS_PER_STEP
    assert rows_a % tt == 0 and 0 < rows_a < t
    n_first = rows_a // tt
    assert s1.shape == (PEER_HEADS, nc, I1_PER_STEP, t)
    s2 = s2.reshape(PEER_HEADS, N_KEYS, t)
    e2 = e2.reshape(PEER_HEADS, N_KEYS, t)
    return pl.pallas_call(
        functools.partial(_expert_kernel, n_first),
        grid=(t // tt, nc),
        in_specs=[
            pl.BlockSpec((D_MODEL, tt), lambda i, c: (0, i)),
            pl.BlockSpec((EXPERTS_PER_STEP, D_MODEL), lambda i, c: (c, 0)),
            pl.BlockSpec((D_MODEL, EXPERTS_PER_STEP), lambda i, c: (0, c)),
            pl.BlockSpec((PEER_HEADS, 1, I1_PER_STEP, tt), lambda i, c: (0, c, 0, i)),
            pl.BlockSpec((PEER_HEADS, 1, I1_PER_STEP, tt), lambda i, c: (0, c, 0, i)),
            pl.BlockSpec((PEER_HEADS, N_KEYS, tt), lambda i, c: (0, 0, i)),
            pl.BlockSpec((PEER_HEADS, N_KEYS, tt), lambda i, c: (0, 0, i)),
            pl.BlockSpec((PEER_HEADS, tt), lambda i, c: (0, i)),
            pl.BlockSpec((PEER_HEADS, tt), lambda i, c: (0, i)),
            pl.BlockSpec((tt, D_MODEL), lambda i, c: (i, 0)),
            pl.BlockSpec((1, D_MODEL), lambda i, c: (0, 0)),
        ],
        out_specs=[
            pl.BlockSpec((tt, D_MODEL), lambda i, c: (jnp.minimum(i, n_first - 1), 0)),
            pl.BlockSpec((tt, D_MODEL), lambda i, c: (jnp.maximum(i - n_first, 0), 0)),
        ],
        out_shape=[
            jax.ShapeDtypeStruct((rows_a, D_MODEL), F32),
            jax.ShapeDtypeStruct((t - rows_a, D_MODEL), F32),
        ],
        scratch_shapes=[
            pltpu.VMEM((D_MODEL, tt), F32),
            pltpu.VMEM((EXPERTS_PER_STEP, tt), BF16),
            pltpu.VMEM((2, N_KEYS, tt), F32),
            pltpu.VMEM((PEER_HEADS, I1_PER_STEP, tt), F32),
        ],
        compiler_params=_cparams(("arbitrary", "arbitrary")),
        name="peer_experts",
    )(hn_t, u_bf, v_t, s1, e1, s2, e2, tau, zinv, x2, nfw)


def _pick_tile(t, prefs):
    for p in prefs:
        if t % p == 0:
            return p
    raise ValueError(f"token count {t} not tileable by {prefs}")


def _layer(x_all, nseq_p, nblk, ca0, cs0, ss0, lw, norm_final_w):
    (norm_mix_w, w_in, conv_a_w, w_a_out, ssd_conv_w, ssd_conv_b, ssd_dt_bias, ssd_a_log, ssd_d,
     ssd_norm_w, w_ssd_out, w_o, norm_ffn_w, peer_w_q, peer_sub_keys, peer_u, peer_v) = lw
    t_all = x_all.shape[0]
    t_real = nseq_p * (nblk - 1) * ROWS
    nseq_s = ca0.shape[0]
    t_s = nseq_s * 8

    c = np.cumsum([0, 1024, 1024, 1024, SSD_INNER, SSD_CONV_DIM, SSD_HEADS, 1024, 1024])
    w_in_r = jnp.concatenate(
        [w_in[:, c[0]:c[5]], w_in[:, c[6]:c[8]], w_in[:, c[5]:c[6]],
         jnp.zeros((D_MODEL, IN_DIM_R - COL_DT - SSD_HEADS), w_in.dtype)], axis=1).astype(BF16)
    pad_h = lambda v: jnp.pad(v.astype(F32), (0, LANES - SSD_HEADS)).reshape(1, LANES)
    mixer_w = (conv_a_w.astype(F32), ssd_conv_w.astype(F32), ssd_conv_b.reshape(1, -1).astype(F32),
               pad_h(ssd_dt_bias), pad_h(ssd_a_log),
               jnp.repeat(ssd_d.astype(F32), SSD_HEADDIM).reshape(1, SSD_INNER),
               ssd_norm_w.reshape(1, SSD_INNER).astype(F32))

    proj = _inproj(x_all, norm_mix_w.reshape(1, D_MODEL), w_in_r,
                   _pick_tile(t_all, (1024, 512, 256)), INPROJ_NT)
    ya_p, yn_p, nca_p, ncs_p, nss_p = _mixer_prompt(proj, mixer_w, nseq_p, nblk, t_real + t_s)
    ya_s, yn_s, nca_s, ncs_s, nss_s = _mixer_sample(
        proj, ca0, cs0, ss0.reshape(nseq_s, SSD_INNER, SSD_STATE), mixer_w, nseq_s, t_real)

    x2, hn_t = _merge(x_all, ya_p, ya_s, yn_p, yn_s, proj, w_a_out.astype(BF16), w_ssd_out.astype(BF16),
                      w_o.astype(BF16), norm_ffn_w.reshape(1, D_MODEL),
                      _pick_tile(math.gcd(t_real, t_s), (512, 256)), t_real)

    wq_t = peer_w_q.T.astype(BF16)
    keys = peer_sub_keys.reshape(PEER_HEADS * 2, N_KEYS, PEER_HALF).astype(BF16)
    s1, e1, s2, e2, tau, zinv = _route(hn_t, wq_t, keys, _pick_tile(t_all, (256, 128)))
    y_real, y_rest = _experts(hn_t, peer_u.astype(BF16), peer_v.T.astype(BF16), s1, e1, s2, e2, tau,
                              zinv, x2, norm_final_w.reshape(1, D_MODEL),
                              _pick_tile(t_all, (256, 128)), t_real)
    shp = (SSD_HEADS, SSD_HEADDIM, SSD_STATE)
    return (y_real, y_rest[:t_s], nca_p, ncs_p, nss_p.reshape(nseq_p, *shp),
            nca_s, ncs_s, nss_s.reshape(nseq_s, *shp))


def kernel(x_prompt, x_sample, state_conv_a, state_conv_ssd, state_ssm, meta_tokens, norm_mix_w, w_in, conv_a_w, w_a_out, ssd_conv_w, ssd_conv_b, ssd_dt_bias, ssd_a_log, ssd_d, ssd_norm_w, w_ssd_out, w_o, norm_ffn_w, peer_w_q, peer_sub_keys, peer_u, peer_v, norm_final_w):
    bp, s_len, d = x_prompt.shape
    bs, ds_len, _ = x_sample.shape
    depth = w_in.shape[0]
    assert depth == 1 and d == D_MODEL and ds_len == 8 and s_len % ROWS == 0
    nblk = s_len // ROWS + 1
    head = jnp.concatenate(
        [jnp.zeros((ROWS - N_META, d), x_prompt.dtype), meta_tokens.astype(x_prompt.dtype)], axis=0)
    x_all = jnp.concatenate(
        [x_prompt.reshape(-1, d), x_sample.reshape(-1, d),
         jnp.broadcast_to(head[None], (bp, ROWS, d)).reshape(-1, d)], axis=0)
    sq = lambda a: a.reshape(a.shape[1:])
    lw = tuple(sq(a) for a in (
        norm_mix_w, w_in, conv_a_w, w_a_out, ssd_conv_w, ssd_conv_b, ssd_dt_bias, ssd_a_log,
        ssd_d, ssd_norm_w, w_ssd_out, w_o, norm_ffn_w, peer_w_q, peer_sub_keys, peer_u, peer_v))
    y_real, y_s, nca_p, ncs_p, nss_p, nca_s, ncs_s, nss_s = _layer(
        x_all, bp, nblk, sq(state_conv_a), sq(state_conv_ssd), sq(state_ssm), lw, norm_final_w)
    y_prompt = y_real.reshape(bp, s_len, d)
    y_sample = y_s.reshape(bs, ds_len, d)
    return (y_prompt, y_sample, nca_p[None], ncs_p[None], nss_p[None],
            nca_s[None], ncs_s[None], nss_s[None])
```

```python
import functools
import math

import numpy as np
import jax
import jax.numpy as jnp
from jax import lax
from jax.experimental import pallas as pl
from jax.experimental.pallas import tpu as pltpu

F32 = jnp.float32
BF16 = jnp.bfloat16

D_MODEL = 1024
N_META = 16
NORM_EPS = 1e-6
CONV_A_W = 3
SSD_INNER = 2048
SSD_HEADDIM = 64
SSD_HEADS = 32
SSD_STATE = 128
SSD_GROUPS = 4
SSD_HPG = 8
SSD_CONV_W = 4
SSD_CONV_DIM = SSD_INNER + 2 * SSD_GROUPS * SSD_STATE
PEER_HEADS = 8
N_KEYS = 128
PEER_TOPK = 16
PEER_HALF = 128

ROWS = 128
LANES = 128
SUBLANES = 8
BF16_ROWS = 16
COL_AB, COL_AC, COL_AX, COL_Z, COL_XBC = 0, 1024, 2048, 3072, 5120
COL_G = 8192
COL_DT = 10240
MXU_COLS = 256
INPROJ_NT = 7 * MXU_COLS
IN_DIM_R = 6 * INPROJ_NT
assert IN_DIM_R >= COL_DT + LANES
MAIN_W = COL_G
NEG_INF = float("-inf")

VMEM_LIMIT = 56 * 1024 * 1024


def _cparams(sem, flags=None):
    return pltpu.CompilerParams(dimension_semantics=sem, vmem_limit_bytes=VMEM_LIMIT, flags=flags)


def _inproj_kernel(x_ref, nw_ref, w_ref, o_ref, hn_ref):
    @pl.when(pl.program_id(1) == 0)
    def _():
        x = x_ref[...]
        inv = lax.rsqrt(jnp.mean(x * x, axis=-1, keepdims=True) + NORM_EPS)
        hn_ref[...] = ((x * inv) * nw_ref[...]).astype(BF16)

    o_ref[...] = jnp.dot(hn_ref[...], w_ref[...], preferred_element_type=F32)


def _inproj(x_all, norm_w, w_in_r, tt, nt):
    t = x_all.shape[0]
    n = w_in_r.shape[1]
    return pl.pallas_call(
        _inproj_kernel,
        grid=(t // tt, n // nt),
        in_specs=[
            pl.BlockSpec((tt, D_MODEL), lambda i, j: (i, 0)),
            pl.BlockSpec((1, D_MODEL), lambda i, j: (0, 0)),
            pl.BlockSpec((D_MODEL, nt), lambda i, j: (0, j)),
        ],
        out_specs=pl.BlockSpec((tt, nt), lambda i, j: (i, j)),
        out_shape=jax.ShapeDtypeStruct((t, n), F32),
        scratch_shapes=[pltpu.VMEM((tt, D_MODEL), BF16)],
        compiler_params=_cparams(("arbitrary", "arbitrary")),
        name="inproj",
    )(x_all, norm_w, w_in_r)


def _silu(v):
    return v * jax.nn.sigmoid(v)


def _mixer_block(a_b, u, z, xbc, dt_raw, r0, su_ref, sx_ref, s_ref, gated_ref,
                 caw_ref, scw_ref, scb_ref, dtb_ref, alog_ref, dexp_ref, nrm_ref):
    R = ROWS
    su_ref[8:8 + R, :] = u
    u1 = su_ref[7:7 + R, :]
    u2 = su_ref[6:6 + R, :]
    conv_a = caw_ref[2:3, :] * u + caw_ref[1:2, :] * u1 + caw_ref[0:1, :] * u2
    ya = (a_b * conv_a).astype(BF16)
    su_ref[0:8, :] = su_ref[R:R + 8, :]

    sx_ref[8:8 + R, :] = xbc
    x1 = sx_ref[7:7 + R, :]
    x2 = sx_ref[6:6 + R, :]
    x3 = sx_ref[5:5 + R, :]
    xc = (scw_ref[3:4, :] * xbc + scw_ref[2:3, :] * x1 + scw_ref[1:2, :] * x2
          + scw_ref[0:1, :] * x3 + scb_ref[...])
    xc = _silu(xc)
    sx_ref[0:8, :] = sx_ref[R:R + 8, :]

    row = lax.broadcasted_iota(jnp.int32, (R, LANES), 0)
    col = lax.broadcasted_iota(jnp.int32, (R, LANES), 1)
    valid = row >= r0
    dt = jnp.where(valid, jax.nn.softplus(dt_raw + dtb_ref[...]), 0.0)
    a = -jnp.exp(alog_ref[...])
    da = dt * a
    tri = row >= col
    tri_f = tri.astype(F32)
    cum = jnp.dot(tri_f, da, preferred_element_type=F32, precision=lax.Precision.HIGHEST)
    da_t = da.T
    dt_t = dt.T
    triu_f = (row <= col).astype(F32)
    cum_t = jnp.dot(da_t, triu_f, preferred_element_type=F32, precision=lax.Precision.HIGHEST)
    ecum = jnp.exp(cum)
    cum_last = cum[R - 1:R, :]
    wend = jnp.exp(cum_last - cum) * dt
    dec_h = jnp.broadcast_to(jnp.exp(cum_t[:, R - 1:R]), (LANES, LANES))

    lane_lo = col < SSD_HEADDIM
    for g in range(SSD_GROUPS):
        b_g = xc[:, SSD_INNER + g * SSD_STATE: SSD_INNER + (g + 1) * SSD_STATE]
        c_g = xc[:, SSD_INNER + SSD_GROUPS * SSD_STATE + g * SSD_STATE:
                 SSD_INNER + SSD_GROUPS * SSD_STATE + (g + 1) * SSD_STATE]
        b_bf = b_g.astype(BF16)
        c_bf = c_g.astype(BF16)
        cb = lax.dot_general(c_bf, b_bf, (((1,), (1,)), ((), ())), preferred_element_type=F32)
        s_g = s_ref[g * 512:(g + 1) * 512, :]
        yo_g = lax.dot_general(c_bf, s_g.astype(BF16), (((1,), (1,)), ((), ())),
                               preferred_element_type=F32)
        for pr in range(SSD_HPG // 2):
            h0 = g * SSD_HPG + 2 * pr
            h1 = h0 + 1
            c0 = (h0 * SSD_HEADDIM)
            x_pair = xc[:, c0:c0 + LANES]
            lmats = []
            for h in (h0, h1):
                seg = jnp.where(tri, cum[:, h:h + 1] - cum_t[h:h + 1, :], NEG_INF)
                lmats.append((cb * jnp.exp(seg) * dt_t[h:h + 1, :]).astype(BF16))
            x_lo = jnp.where(lane_lo, x_pair, 0.0).astype(BF16)
            x_hi = jnp.where(lane_lo, 0.0, x_pair).astype(BF16)
            yd = (jnp.dot(lmats[0], x_lo, preferred_element_type=F32)
                  + jnp.dot(lmats[1], x_hi, preferred_element_type=F32))
            scal = jnp.where(lane_lo, ecum[:, h0:h0 + 1], ecum[:, h1:h1 + 1])
            yo = yo_g[:, 2 * pr * SSD_HEADDIM: 2 * pr * SSD_HEADDIM + LANES]
            y = yd + yo * scal + dexp_ref[:, c0:c0 + LANES] * x_pair
            zz = z[:, c0:c0 + LANES]
            gated_ref[:, c0:c0 + LANES] = y * _silu(zz)
            wsel = jnp.where(lane_lo, wend[:, h0:h0 + 1], wend[:, h1:h1 + 1])
            xw_t = (x_pair * wsel).T.astype(BF16)
            cs = jnp.dot(xw_t, b_bf, preferred_element_type=F32)
            dec = jnp.concatenate(
                [jnp.broadcast_to(dec_h[h0:h0 + 1, :], (SSD_HEADDIM, LANES)),
                 jnp.broadcast_to(dec_h[h1:h1 + 1, :], (SSD_HEADDIM, LANES))], axis=0)
            r_lo = h0 * SSD_HEADDIM
            s_ref[r_lo:r_lo + LANES, :] = dec * s_ref[r_lo:r_lo + LANES, :] + cs

    outs = []
    for g in range(SSD_GROUPS):
        gg = gated_ref[:, g * 512:(g + 1) * 512]
        ms = jnp.mean(gg * gg, axis=-1, keepdims=True)
        outs.append(((gg * lax.rsqrt(ms + NORM_EPS)).astype(F32)
                     * nrm_ref[:, g * 512:(g + 1) * 512]).astype(BF16))
    yn = jnp.concatenate(outs, axis=1)
    return ya, yn


def _mixer_prompt_kernel(main_ref, dt_ref, caw_ref, scw_ref, scb_ref, dtb_ref, alog_ref,
                         dexp_ref, nrm_ref,
                         ya_ref, yn_ref, nca_ref, ncs_ref, nss_ref,
                         su_ref, sx_ref, s_ref, gated_ref):
    j = pl.program_id(1)

    @pl.when(j == 0)
    def _():
        su_ref[0:8, :] = jnp.zeros((8, D_MODEL), F32)
        sx_ref[0:8, :] = jnp.zeros((8, SSD_CONV_DIM), F32)
        s_ref[...] = jnp.zeros_like(s_ref)

    a_b = main_ref[:, COL_AB:COL_AB + 1024]
    u = main_ref[:, COL_AC:COL_AC + 1024] * main_ref[:, COL_AX:COL_AX + 1024]
    z = main_ref[:, COL_Z:COL_Z + 2048]
    xbc = main_ref[:, COL_XBC:COL_XBC + SSD_CONV_DIM]
    r0 = jnp.where(j == 0, ROWS - N_META, 0)
    ya, yn = _mixer_block(a_b, u, z, xbc, dt_ref[...], r0, su_ref, sx_ref, s_ref, gated_ref,
                          caw_ref, scw_ref, scb_ref, dtb_ref, alog_ref, dexp_ref, nrm_ref)
    ya_ref[...] = ya
    yn_ref[...] = yn
    nca_ref[0] = su_ref[6:8, :]
    ncs_ref[0] = sx_ref[5:8, :]
    nss_ref[0] = s_ref[...]


def _mixer_sample_kernel(main_ref, dt_ref, ca0_ref, cs0_ref, ss0_ref,
                         caw_ref, scw_ref, scb_ref, dtb_ref, alog_ref, dexp_ref, nrm_ref,
                         ya_ref, yn_ref, nca_ref, ncs_ref, nss_ref,
                         su_ref, sx_ref, s_ref, gated_ref, pad_ref, tmp_ref):
    R = ROWS
    n = 8
    su_ref[0:8, :] = jnp.zeros((8, D_MODEL), F32)
    sx_ref[0:8, :] = jnp.zeros((8, SSD_CONV_DIM), F32)
    s_ref[...] = ss0_ref[0]

    pad_ref[...] = jnp.zeros_like(pad_ref)
    pad_ref[R - n:R, :] = main_ref[...]
    a_b = pad_ref[:, COL_AB:COL_AB + 1024]
    z = pad_ref[:, COL_Z:COL_Z + 2048]
    tmp_ref[...] = jnp.zeros_like(tmp_ref)
    tmp_ref[6:8, 0:D_MODEL] = ca0_ref[0]
    u = jnp.concatenate(
        [jnp.zeros((R - 2 * n, D_MODEL), F32), tmp_ref[:, 0:D_MODEL],
         main_ref[:, COL_AC:COL_AC + 1024] * main_ref[:, COL_AX:COL_AX + 1024]], axis=0)
    tmp_ref[...] = jnp.zeros_like(tmp_ref)
    tmp_ref[5:8, :] = cs0_ref[0]
    xbc = jnp.concatenate(
        [jnp.zeros((R - 2 * n, SSD_CONV_DIM), F32), tmp_ref[...],
         main_ref[:, COL_XBC:COL_XBC + SSD_CONV_DIM]], axis=0)
    dt_raw = jnp.concatenate([jnp.zeros((R - n, LANES), F32), dt_ref[...]], axis=0)
    ya, yn = _mixer_block(a_b, u, z, xbc, dt_raw, R - n, su_ref, sx_ref, s_ref, gated_ref,
                          caw_ref, scw_ref, scb_ref, dtb_ref, alog_ref, dexp_ref, nrm_ref)
    ya_ref[...] = ya[R - n:R, :]
    yn_ref[...] = yn[R - n:R, :]
    nca_ref[0] = su_ref[6:8, :]
    ncs_ref[0] = sx_ref[5:8, :]
    nss_ref[0] = s_ref[...]


def _mixer_weight_specs(nidx):
    zero = (lambda *a: (0, 0))
    del nidx
    return [
        pl.BlockSpec((CONV_A_W, D_MODEL), zero),
        pl.BlockSpec((SSD_CONV_W, SSD_CONV_DIM), zero),
        pl.BlockSpec((1, SSD_CONV_DIM), zero),
        pl.BlockSpec((1, LANES), zero),
        pl.BlockSpec((1, LANES), zero),
        pl.BlockSpec((1, SSD_INNER), zero),
        pl.BlockSpec((1, SSD_INNER), zero),
    ]


def _mixer_scratch():
    return [
        pltpu.VMEM((ROWS + 8, D_MODEL), F32),
        pltpu.VMEM((ROWS + 8, SSD_CONV_DIM), F32),
        pltpu.VMEM((SSD_INNER, SSD_STATE), F32),
        pltpu.VMEM((ROWS, SSD_INNER), F32),
    ]


def _mixer_prompt(proj, weights, nseq, nblk, meta_row0):
    dt_blk = COL_DT // LANES
    meta_blk0 = meta_row0 // ROWS
    rblk = lambda b, j: jnp.where(j == 0, meta_blk0 + b, b * (nblk - 1) + j - 1)
    return pl.pallas_call(
        _mixer_prompt_kernel,
        grid=(nseq, nblk),
        in_specs=[
            pl.BlockSpec((ROWS, MAIN_W), lambda b, j: (rblk(b, j), 0)),
            pl.BlockSpec((ROWS, LANES), lambda b, j: (rblk(b, j), dt_blk)),
        ] + _mixer_weight_specs(2),
        out_specs=[
            pl.BlockSpec((ROWS, D_MODEL), lambda b, j: (rblk(b, j), 0)),
            pl.BlockSpec((ROWS, SSD_INNER), lambda b, j: (rblk(b, j), 0)),
            pl.BlockSpec((1, CONV_A_W - 1, D_MODEL), lambda b, j: (b, 0, 0)),
            pl.BlockSpec((1, SSD_CONV_W - 1, SSD_CONV_DIM), lambda b, j: (b, 0, 0)),
            pl.BlockSpec((1, SSD_INNER, SSD_STATE), lambda b, j: (b, 0, 0)),
        ],
        out_shape=[
            jax.ShapeDtypeStruct((proj.shape[0], D_MODEL), BF16),
            jax.ShapeDtypeStruct((proj.shape[0], SSD_INNER), BF16),
            jax.ShapeDtypeStruct((nseq, CONV_A_W - 1, D_MODEL), F32),
            jax.ShapeDtypeStruct((nseq, SSD_CONV_W - 1, SSD_CONV_DIM), F32),
            jax.ShapeDtypeStruct((nseq, SSD_INNER, SSD_STATE), F32),
        ],
        scratch_shapes=_mixer_scratch(),
        compiler_params=_cparams(("arbitrary", "arbitrary")),
        name="mixer_prompt",
    )(proj, proj, *weights)


def _mixer_sample(proj, ca0, cs0, ss0, weights, nseq, row0):
    dt_blk = COL_DT // LANES
    blk0 = row0 // 8
    return pl.pallas_call(
        _mixer_sample_kernel,
        grid=(nseq,),
        in_specs=[
            pl.BlockSpec((8, MAIN_W), lambda s: (blk0 + s, 0)),
            pl.BlockSpec((8, LANES), lambda s: (blk0 + s, dt_blk)),
            pl.BlockSpec((1, CONV_A_W - 1, D_MODEL), lambda s: (s, 0, 0)),
            pl.BlockSpec((1, SSD_CONV_W - 1, SSD_CONV_DIM), lambda s: (s, 0, 0)),
            pl.BlockSpec((1, SSD_INNER, SSD_STATE), lambda s: (s, 0, 0)),
        ] + _mixer_weight_specs(1),
        out_specs=[
            pl.BlockSpec((8, D_MODEL), lambda s: (s, 0)),
            pl.BlockSpec((8, SSD_INNER), lambda s: (s, 0)),
            pl.BlockSpec((1, CONV_A_W - 1, D_MODEL), lambda s: (s, 0, 0)),
            pl.BlockSpec((1, SSD_CONV_W - 1, SSD_CONV_DIM), lambda s: (s, 0, 0)),
            pl.BlockSpec((1, SSD_INNER, SSD_STATE), lambda s: (s, 0, 0)),
        ],
        out_shape=[
            jax.ShapeDtypeStruct((nseq * 8, D_MODEL), BF16),
            jax.ShapeDtypeStruct((nseq * 8, SSD_INNER), BF16),
            jax.ShapeDtypeStruct((nseq, CONV_A_W - 1, D_MODEL), F32),
            jax.ShapeDtypeStruct((nseq, SSD_CONV_W - 1, SSD_CONV_DIM), F32),
            jax.ShapeDtypeStruct((nseq, SSD_INNER, SSD_STATE), F32),
        ],
        scratch_shapes=_mixer_scratch() + [
            pltpu.VMEM((ROWS, MAIN_W), F32),
            pltpu.VMEM((8, SSD_CONV_DIM), F32),
        ],
        compiler_params=_cparams(("arbitrary",)),
        name="mixer_sample",
    )(proj, proj, ca0, cs0, ss0, *weights)


def _merge_kernel(s_lo, s_hi, x_ref, yap_ref, yas_ref, ynp_ref, yns_ref, g_ref, wa_ref, ws_ref,
                  wo_ref, nw_ref, x2_ref, hn_ref):
    i = pl.program_id(0)
    is_sample = jnp.logical_and(i >= s_lo, i < s_hi)
    ya = jnp.where(is_sample, yas_ref[...], yap_ref[...])
    yn = jnp.where(is_sample, yns_ref[...], ynp_ref[...])
    br_a = jnp.dot(ya, wa_ref[...], preferred_element_type=F32)
    br_b = jnp.dot(yn, ws_ref[...], preferred_element_type=F32)
    g = g_ref[...]
    merged = jax.nn.sigmoid(g[:, :D_MODEL]) * br_a + jax.nn.sigmoid(g[:, D_MODEL:]) * br_b
    mix = jnp.dot(merged.astype(BF16), wo_ref[...], preferred_element_type=F32)
    x2 = x_ref[...] + mix
    x2_ref[...] = x2
    inv = lax.rsqrt(jnp.mean(x2 * x2, axis=-1, keepdims=True) + NORM_EPS)
    hn_ref[...] = ((x2 * inv) * nw_ref[...]).T.astype(BF16)


def _merge(x_all, ya_p, ya_s, yn_p, yn_s, proj, wa, ws, wo, nw, tt, sample_row0):
    t = x_all.shape[0]
    assert sample_row0 % tt == 0 and ya_s.shape[0] % tt == 0 and ya_p.shape[0] == t
    s_lo = sample_row0 // tt
    s_hi = s_lo + ya_s.shape[0] // tt
    in_s = lambda i: jnp.logical_and(i >= s_lo, i < s_hi)
    p_map = lambda i: (jnp.where(in_s(i), 0, i), 0)
    s_map = lambda i: (jnp.clip(i - s_lo, 0, s_hi - s_lo - 1), 0)
    gblk = COL_G // (2 * D_MODEL)
    zero = lambda i: (0, 0)
    return pl.pallas_call(
        functools.partial(_merge_kernel, s_lo, s_hi),
        grid=(t // tt,),
        in_specs=[
            pl.BlockSpec((tt, D_MODEL), lambda i: (i, 0)),
            pl.BlockSpec((tt, D_MODEL), p_map),
            pl.BlockSpec((tt, D_MODEL), s_map),
            pl.BlockSpec((tt, SSD_INNER), p_map),
            pl.BlockSpec((tt, SSD_INNER), s_map),
            pl.BlockSpec((tt, 2 * D_MODEL), lambda i: (i, gblk)),
            pl.BlockSpec((D_MODEL, D_MODEL), zero),
            pl.BlockSpec((SSD_INNER, D_MODEL), zero),
            pl.BlockSpec((D_MODEL, D_MODEL), zero),
            pl.BlockSpec((1, D_MODEL), zero),
        ],
        out_specs=[
            pl.BlockSpec((tt, D_MODEL), lambda i: (i, 0)),
            pl.BlockSpec((D_MODEL, tt), lambda i: (0, i)),
        ],
        out_shape=[
            jax.ShapeDtypeStruct((t, D_MODEL), F32),
            jax.ShapeDtypeStruct((D_MODEL, t), BF16),
        ],
        compiler_params=_cparams(("arbitrary",)),
        name="merge",
    )(x_all, ya_p, ya_s, yn_p, yn_s, proj, wa, ws, wo, nw)


def _oddeven_merge_sort_pairs(n):
    pairs = []
    p = 1
    while p < n:
        k = p
        while k >= 1:
            j = k % p
            while j <= n - 1 - k:
                for i in range(min(k, n - j - k)):
                    if (i + j) // (2 * p) == (i + j + k) // (2 * p):
                        pairs.append((i + j, i + j + k))
                j += 2 * k
            k //= 2
        p *= 2
    return pairs


_SORT16 = _oddeven_merge_sort_pairs(16)
_BITONIC16 = [(i, i + d) for d in (8, 4, 2, 1) for i in range(16) if (i & d) == 0]
_CAND_PAIRS = [(r1, r2) for r1 in range(PEER_TOPK) for r2 in range(PEER_TOPK)
               if (r1 + 1) * (r2 + 1) <= PEER_TOPK]


def _ce(v, i, j):
    hi = jnp.maximum(v[i], v[j])
    lo = jnp.minimum(v[i], v[j])
    v[i], v[j] = hi, lo


def _top16_desc(v):
    v = list(v)
    for i, j in _SORT16:
        _ce(v, i, j)
    for shift in (4, 2, 1):
        p = [pltpu.roll(x, shift, 0) for x in v]
        v = [jnp.maximum(v[r], p[15 - r]) for r in range(16)]
        for i, j in _BITONIC16:
            _ce(v, i, j)
    return v


def _route_kernel(ht_ref, wq_ref, keys_ref, s1_ref, e1_ref, s2_ref, e2_ref, tau_ref, zinv_ref, q_ref):
    tt = ht_ref.shape[1]
    q_ref[...] = jnp.dot(wq_ref[...], ht_ref[...], preferred_element_type=F32).astype(BF16)
    sub = lax.broadcasted_iota(jnp.int32, (SUBLANES, tt), 0)
    tops = [[None] * PEER_TOPK, [None] * PEER_TOPK]
    s_refs, e_refs = (s1_ref, s2_ref), (e1_ref, e2_ref)
    for h in range(PEER_HEADS):
        for i in range(2):
            r0 = (h * 2 + i) * PEER_HALF
            s = jnp.dot(keys_ref[h * 2 + i], q_ref[r0:r0 + PEER_HALF, :],
                        preferred_element_type=F32)
            v = [s[8 * j:8 * j + 8, :] for j in range(16)]
            top = _top16_desc(v)
            thr, best = top[PEER_TOPK - 1], top[0]
            for j in range(16):
                s_refs[i][h, j] = jnp.where(v[j] >= thr, v[j], NEG_INF)
                e_refs[i][h, j] = jnp.exp(v[j] - best)
            for r in range(PEER_TOPK):
                tops[i][r] = top[r] if h == 0 else jnp.where(sub == h, top[r], tops[i][r])
    cand = [tops[0][r1] + tops[1][r2] for r1, r2 in _CAND_PAIRS]
    picked = []
    for _ in range(PEER_TOPK):
        m = cand[0]
        for c in cand[1:]:
            m = jnp.maximum(m, c)
        picked.append(m)
        found = jnp.zeros(m.shape, jnp.bool_)
        nxt = []
        for c in cand:
            eq = c == m
            take = jnp.logical_and(eq, jnp.logical_not(found))
            found = jnp.logical_or(found, eq)
            nxt.append(jnp.where(take, NEG_INF, c))
        cand = nxt
    zsum = jnp.zeros_like(picked[0])
    for c in picked:
        zsum = zsum + jnp.exp(c - picked[0])
    tau_ref[...] = picked[PEER_TOPK - 1]
    zinv_ref[...] = 1.0 / zsum


def _route(hn_t, wq_t, keys, tt):
    t = hn_t.shape[1]
    tab_shape = (PEER_HEADS, N_KEYS // SUBLANES, SUBLANES, t)
    tab_spec = pl.BlockSpec((PEER_HEADS, N_KEYS // SUBLANES, SUBLANES, tt), lambda i: (0, 0, 0, i))
    return pl.pallas_call(
        _route_kernel,
        grid=(t // tt,),
        in_specs=[
            pl.BlockSpec((D_MODEL, tt), lambda i: (0, i)),
            pl.BlockSpec(wq_t.shape, lambda i: (0, 0)),
            pl.BlockSpec(keys.shape, lambda i: (0, 0, 0)),
        ],
        out_specs=[tab_spec] * 4 + [
            pl.BlockSpec((PEER_HEADS, tt), lambda i: (0, i)),
            pl.BlockSpec((PEER_HEADS, tt), lambda i: (0, i)),
        ],
        out_shape=[jax.ShapeDtypeStruct(tab_shape, F32)] * 4 + [
            jax.ShapeDtypeStruct((PEER_HEADS, t), F32),
            jax.ShapeDtypeStruct((PEER_HEADS, t), F32),
        ],
        scratch_shapes=[pltpu.VMEM((wq_t.shape[0], tt), BF16)],
        compiler_params=_cparams(("arbitrary",)),
        name="peer_route",
    )(hn_t, wq_t, keys)


I1_PER_STEP = 16
I1_GROUPS = I1_PER_STEP // SUBLANES
I1_GROUP = 4
EXPERTS_PER_STEP = I1_PER_STEP * N_KEYS


def _expert_kernel(n_first, ht_ref, u_ref, vt_ref, s1_ref, e1_ref, s2_ref, e2_ref, tau_ref,
                   zinv_ref, x2_ref, nfw_ref, ya_ref, yb_ref, acc_ref, w_ref, act_ref, e1z_ref):
    c = pl.program_id(1)
    tt = ht_ref.shape[1]

    @pl.when(c == 0)
    def _():
        acc_ref[...] = jnp.zeros_like(acc_ref)

    for h in range(PEER_HEADS):
        for gi in range(I1_GROUPS):
            e1z_ref[h, gi * SUBLANES:(gi + 1) * SUBLANES, :] = (
                e1_ref[h, gi] * (0.5 * zinv_ref[h:h + 1, :]))

    for j in range(I1_PER_STEP):
        act_ref[j % 2] = jnp.dot(u_ref[j * N_KEYS:(j + 1) * N_KEYS, :], ht_ref[...],
                                 preferred_element_type=F32)
        for half in range(tt // LANES):
            ls = pl.ds(half * LANES, LANES)
            taub = [jnp.broadcast_to(tau_ref[h:h + 1, ls], (BF16_ROWS, LANES))
                    for h in range(PEER_HEADS)]
            s1b = [jnp.broadcast_to(s1_ref[h, j // SUBLANES, j % SUBLANES:j % SUBLANES + 1, ls],
                                    (BF16_ROWS, LANES))
                   for h in range(PEER_HEADS)]
            e1b = [jnp.broadcast_to(e1z_ref[h, j:j + 1, ls], (BF16_ROWS, LANES))
                   for h in range(PEER_HEADS)]
            for bb in range(N_KEYS // BF16_ROWS):
                r = bb * BF16_ROWS
                g = None
                for h in range(PEER_HEADS):
                    v = s1b[h] + s2_ref[h, r:r + BF16_ROWS, ls]
                    t = jnp.where(v >= taub[h], e1b[h] * e2_ref[h, r:r + BF16_ROWS, ls], 0.0)
                    g = t if g is None else g + t
                a = act_ref[j % 2, r:r + BF16_ROWS, ls]
                ge = a * (1.0 + lax.erf(a * math.sqrt(0.5)))
                w_ref[j * N_KEYS + r:j * N_KEYS + r + BF16_ROWS, ls] = (ge * g).astype(BF16)
        if j % I1_GROUP == I1_GROUP - 1:
            k0 = (j + 1 - I1_GROUP) * N_KEYS
            acc_ref[...] += jnp.dot(vt_ref[:, k0:k0 + I1_GROUP * N_KEYS],
                                    w_ref[k0:k0 + I1_GROUP * N_KEYS, :], preferred_element_type=F32)

    @pl.when(c == pl.num_programs(1) - 1)
    def _():
        x3 = x2_ref[...] + acc_ref[...].T
        inv = lax.rsqrt(jnp.mean(x3 * x3, axis=-1, keepdims=True) + NORM_EPS)
        y = (x3 * inv) * nfw_ref[...]
        first = pl.program_id(0) < n_first

        @pl.when(first)
        def _():
            ya_ref[...] = y

        @pl.when(jnp.logical_not(first))
        def _():
            yb_ref[...] = y


def _experts(hn_t, u_bf, v_t, s1, e1, s2, e2, tau, zinv, x2, nfw, tt, rows_a):
    t = hn_t.shape[1]
    n_exp = u_bf.shape[0]
    nc = n_exp // EXPERTS_PER_STEP
    assert rows_a % tt == 0 and 0 < rows_a < t
    n_first = rows_a // tt
    assert s1.shape == (PEER_HEADS, nc * I1_GROUPS, SUBLANES, t)
    s2 = s2.reshape(PEER_HEADS, N_KEYS, t)
    e2 = e2.reshape(PEER_HEADS, N_KEYS, t)
    return pl.pallas_call(
        functools.partial(_expert_kernel, n_first),
        grid=(t // tt, nc),
        in_specs=[
            pl.BlockSpec((D_MODEL, tt), lambda i, c: (0, i)),
            pl.BlockSpec((EXPERTS_PER_STEP, D_MODEL), lambda i, c: (c, 0)),
            pl.BlockSpec((D_MODEL, EXPERTS_PER_STEP), lambda i, c: (0, c)),
            pl.BlockSpec((PEER_HEADS, I1_GROUPS, SUBLANES, tt), lambda i, c: (0, c, 0, i)),
            pl.BlockSpec((PEER_HEADS, I1_GROUPS, SUBLANES, tt), lambda i, c: (0, c, 0, i)),
            pl.BlockSpec((PEER_HEADS, N_KEYS, tt), lambda i, c: (0, 0, i)),
            pl.BlockSpec((PEER_HEADS, N_KEYS, tt), lambda i, c: (0, 0, i)),
            pl.BlockSpec((PEER_HEADS, tt), lambda i, c: (0, i)),
            pl.BlockSpec((PEER_HEADS, tt), lambda i, c: (0, i)),
            pl.BlockSpec((tt, D_MODEL), lambda i, c: (i, 0)),
            pl.BlockSpec((1, D_MODEL), lambda i, c: (0, 0)),
        ],
        out_specs=[
            pl.BlockSpec((tt, D_MODEL), lambda i, c: (jnp.minimum(i, n_first - 1), 0)),
            pl.BlockSpec((tt, D_MODEL), lambda i, c: (jnp.maximum(i - n_first, 0), 0)),
        ],
        out_shape=[
            jax.ShapeDtypeStruct((rows_a, D_MODEL), F32),
            jax.ShapeDtypeStruct((t - rows_a, D_MODEL), F32),
        ],
        scratch_shapes=[
            pltpu.VMEM((D_MODEL, tt), F32),
            pltpu.VMEM((EXPERTS_PER_STEP, tt), BF16),
            pltpu.VMEM((2, N_KEYS, tt), F32),
            pltpu.VMEM((PEER_HEADS, I1_PER_STEP, tt), F32),
        ],
        compiler_params=_cparams(("arbitrary", "arbitrary")),
        name="peer_experts",
    )(hn_t, u_bf, v_t, s1, e1, s2, e2, tau, zinv, x2, nfw)


def _pick_tile(t, prefs):
    for p in prefs:
        if t % p == 0:
            return p
    raise ValueError(f"token count {t} not tileable by {prefs}")


def _layer(x_all, nseq_p, nblk, ca0, cs0, ss0, lw, norm_final_w):
    (norm_mix_w, w_in, conv_a_w, w_a_out, ssd_conv_w, ssd_conv_b, ssd_dt_bias, ssd_a_log, ssd_d,
     ssd_norm_w, w_ssd_out, w_o, norm_ffn_w, peer_w_q, peer_sub_keys, peer_u, peer_v) = lw
    t_all = x_all.shape[0]
    t_real = nseq_p * (nblk - 1) * ROWS
    nseq_s = ca0.shape[0]
    t_s = nseq_s * 8

    c = np.cumsum([0, 1024, 1024, 1024, SSD_INNER, SSD_CONV_DIM, SSD_HEADS, 1024, 1024])
    w_in_r = jnp.concatenate(
        [w_in[:, c[0]:c[5]], w_in[:, c[6]:c[8]], w_in[:, c[5]:c[6]],
         jnp.zeros((D_MODEL, IN_DIM_R - COL_DT - SSD_HEADS), w_in.dtype)], axis=1).astype(BF16)
    pad_h = lambda v: jnp.pad(v.astype(F32), (0, LANES - SSD_HEADS)).reshape(1, LANES)
    mixer_w = (conv_a_w.astype(F32), ssd_conv_w.astype(F32), ssd_conv_b.reshape(1, -1).astype(F32),
               pad_h(ssd_dt_bias), pad_h(ssd_a_log),
               jnp.repeat(ssd_d.astype(F32), SSD_HEADDIM).reshape(1, SSD_INNER),
               ssd_norm_w.reshape(1, SSD_INNER).astype(F32))

    proj = _inproj(x_all, norm_mix_w.reshape(1, D_MODEL), w_in_r,
                   _pick_tile(t_all, (1024, 512, 256)), INPROJ_NT)
    ya_p, yn_p, nca_p, ncs_p, nss_p = _mixer_prompt(proj, mixer_w, nseq_p, nblk, t_real + t_s)
    ya_s, yn_s, nca_s, ncs_s, nss_s = _mixer_sample(
        proj, ca0, cs0, ss0.reshape(nseq_s, SSD_INNER, SSD_STATE), mixer_w, nseq_s, t_real)

    x2, hn_t = _merge(x_all, ya_p, ya_s, yn_p, yn_s, proj, w_a_out.astype(BF16), w_ssd_out.astype(BF16),
                      w_o.astype(BF16), norm_ffn_w.reshape(1, D_MODEL),
                      _pick_tile(math.gcd(t_real, t_s), (512, 256)), t_real)

    wq_t = peer_w_q.T.astype(BF16)
    keys = peer_sub_keys.reshape(PEER_HEADS * 2, N_KEYS, PEER_HALF).astype(BF16)
    s1, e1, s2, e2, tau, zinv = _route(hn_t, wq_t, keys, _pick_tile(t_all, (256, 128)))
    y_real, y_rest = _experts(hn_t, peer_u.astype(BF16), peer_v.T.astype(BF16), s1, e1, s2, e2, tau,
                              zinv, x2, norm_final_w.reshape(1, D_MODEL),
                              _pick_tile(t_all, (256, 128)), t_real)
    shp = (SSD_HEADS, SSD_HEADDIM, SSD_STATE)
    return (y_real, y_rest[:t_s], nca_p, ncs_p, nss_p.reshape(nseq_p, *shp),
            nca_s, ncs_s, nss_s.reshape(nseq_s, *shp))


def kernel(x_prompt, x_sample, state_conv_a, state_conv_ssd, state_ssm, meta_tokens, norm_mix_w, w_in, conv_a_w, w_a_out, ssd_conv_w, ssd_conv_b, ssd_dt_bias, ssd_a_log, ssd_d, ssd_norm_w, w_ssd_out, w_o, norm_ffn_w, peer_w_q, peer_sub_keys, peer_u, peer_v, norm_final_w):
    bp, s_len, d = x_prompt.shape
    bs, ds_len, _ = x_sample.shape
    depth = w_in.shape[0]
    assert depth == 1 and d == D_MODEL and ds_len == 8 and s_len % ROWS == 0
    nblk = s_len // ROWS + 1
    head = jnp.concatenate(
        [jnp.zeros((ROWS - N_META, d), x_prompt.dtype), meta_tokens.astype(x_prompt.dtype)], axis=0)
    x_all = jnp.concatenate(
        [x_prompt.reshape(-1, d), x_sample.reshape(-1, d),
         jnp.broadcast_to(head[None], (bp, ROWS, d)).reshape(-1, d)], axis=0)
    sq = lambda a: a.reshape(a.shape[1:])
    lw = tuple(sq(a) for a in (
        norm_mix_w, w_in, conv_a_w, w_a_out, ssd_conv_w, ssd_conv_b, ssd_dt_bias, ssd_a_log,
        ssd_d, ssd_norm_w, w_ssd_out, w_o, norm_ffn_w, peer_w_q, peer_sub_keys, peer_u, peer_v))
    y_real, y_s, nca_p, ncs_p, nss_p, nca_s, ncs_s, nss_s = _layer(
        x_all, bp, nblk, sq(state_conv_a), sq(state_conv_ssd), sq(state_ssm), lw, norm_final_w)
    y_prompt = y_real.reshape(bp, s_len, d)
    y_sample = y_s.reshape(bs, ds_len, d)
    return (y_prompt, y_sample, nca_p[None], ncs_p[None], nss_p[None],
            nca_s[None], ncs_s[None], nss_s[None])
```

```python
import functools
import math

import numpy as np
import jax
import jax.numpy as jnp
from jax import lax
from jax.experimental import pallas as pl
from jax.experimental.pallas import tpu as pltpu

F32 = jnp.float32
BF16 = jnp.bfloat16

D_MODEL = 1024
N_META = 16
NORM_EPS = 1e-6
CONV_A_W = 3
SSD_INNER = 2048
SSD_HEADDIM = 64
SSD_HEADS = 32
SSD_STATE = 128
SSD_GROUPS = 4
SSD_HPG = 8
SSD_CONV_W = 4
SSD_CONV_DIM = SSD_INNER + 2 * SSD_GROUPS * SSD_STATE
PEER_HEADS = 8
N_KEYS = 128
PEER_TOPK = 16
PEER_HALF = 128

ROWS = 128
LANES = 128
SUBLANES = 8
BF16_ROWS = 16
COL_AB, COL_AC, COL_AX, COL_Z, COL_XBC = 0, 1024, 2048, 3072, 5120
COL_G = 8192
COL_DT = 10240
MXU_COLS = 256
INPROJ_NT = 7 * MXU_COLS
IN_DIM_R = 6 * INPROJ_NT
assert IN_DIM_R >= COL_DT + LANES
MAIN_W = COL_G
NEG_INF = float("-inf")

VMEM_LIMIT = 56 * 1024 * 1024


def _cparams(sem, flags=None):
    return pltpu.CompilerParams(dimension_semantics=sem, vmem_limit_bytes=VMEM_LIMIT, flags=flags)


def _inproj_kernel(x_ref, nw_ref, w_ref, o_ref, hn_ref):
    @pl.when(pl.program_id(1) == 0)
    def _():
        x = x_ref[...]
        inv = lax.rsqrt(jnp.mean(x * x, axis=-1, keepdims=True) + NORM_EPS)
        hn_ref[...] = ((x * inv) * nw_ref[...]).astype(BF16)

    o_ref[...] = jnp.dot(hn_ref[...], w_ref[...], preferred_element_type=F32)


def _inproj(x_all, norm_w, w_in_r, tt, nt):
    t = x_all.shape[0]
    n = w_in_r.shape[1]
    return pl.pallas_call(
        _inproj_kernel,
        grid=(t // tt, n // nt),
        in_specs=[
            pl.BlockSpec((tt, D_MODEL), lambda i, j: (i, 0)),
            pl.BlockSpec((1, D_MODEL), lambda i, j: (0, 0)),
            pl.BlockSpec((D_MODEL, nt), lambda i, j: (0, j)),
        ],
        out_specs=pl.BlockSpec((tt, nt), lambda i, j: (i, j)),
        out_shape=jax.ShapeDtypeStruct((t, n), F32),
        scratch_shapes=[pltpu.VMEM((tt, D_MODEL), BF16)],
        compiler_params=_cparams(("arbitrary", "arbitrary")),
        name="inproj",
    )(x_all, norm_w, w_in_r)


def _silu(v):
    return v * jax.nn.sigmoid(v)


def _mixer_block(a_b, u, z, xbc, dt_raw, r0, su_ref, sx_ref, s_ref, gated_ref,
                 caw_ref, scw_ref, scb_ref, dtb_ref, alog_ref, dexp_ref, nrm_ref):
    R = ROWS
    su_ref[8:8 + R, :] = u
    u1 = su_ref[7:7 + R, :]
    u2 = su_ref[6:6 + R, :]
    conv_a = caw_ref[2:3, :] * u + caw_ref[1:2, :] * u1 + caw_ref[0:1, :] * u2
    ya = (a_b * conv_a).astype(BF16)
    su_ref[0:8, :] = su_ref[R:R + 8, :]

    sx_ref[8:8 + R, :] = xbc
    x1 = sx_ref[7:7 + R, :]
    x2 = sx_ref[6:6 + R, :]
    x3 = sx_ref[5:5 + R, :]
    xc = (scw_ref[3:4, :] * xbc + scw_ref[2:3, :] * x1 + scw_ref[1:2, :] * x2
          + scw_ref[0:1, :] * x3 + scb_ref[...])
    xc = _silu(xc)
    sx_ref[0:8, :] = sx_ref[R:R + 8, :]

    row = lax.broadcasted_iota(jnp.int32, (R, LANES), 0)
    col = lax.broadcasted_iota(jnp.int32, (R, LANES), 1)
    valid = row >= r0
    dt = jnp.where(valid, jax.nn.softplus(dt_raw + dtb_ref[...]), 0.0)
    a = -jnp.exp(alog_ref[...])
    da = dt * a
    tri = row >= col
    tri_f = tri.astype(F32)
    cum = jnp.dot(tri_f, da, preferred_element_type=F32, precision=lax.Precision.HIGHEST)
    da_t = da.T
    dt_t = dt.T
    triu_f = (row <= col).astype(F32)
    cum_t = jnp.dot(da_t, triu_f, preferred_element_type=F32, precision=lax.Precision.HIGHEST)
    ecum = jnp.exp(cum)
    cum_last = cum[R - 1:R, :]
    wend = jnp.exp(cum_last - cum) * dt
    dec_h = jnp.broadcast_to(jnp.exp(cum_t[:, R - 1:R]), (LANES, LANES))

    lane_lo = col < SSD_HEADDIM
    for g in range(SSD_GROUPS):
        b_g = xc[:, SSD_INNER + g * SSD_STATE: SSD_INNER + (g + 1) * SSD_STATE]
        c_g = xc[:, SSD_INNER + SSD_GROUPS * SSD_STATE + g * SSD_STATE:
                 SSD_INNER + SSD_GROUPS * SSD_STATE + (g + 1) * SSD_STATE]
        b_bf = b_g.astype(BF16)
        c_bf = c_g.astype(BF16)
        cb = lax.dot_general(c_bf, b_bf, (((1,), (1,)), ((), ())), preferred_element_type=F32)
        s_g = s_ref[g * 512:(g + 1) * 512, :]
        yo_g = lax.dot_general(c_bf, s_g.astype(BF16), (((1,), (1,)), ((), ())),
                               preferred_element_type=F32)
        for pr in range(SSD_HPG // 2):
            h0 = g * SSD_HPG + 2 * pr
            h1 = h0 + 1
            c0 = (h0 * SSD_HEADDIM)
            x_pair = xc[:, c0:c0 + LANES]
            lmats = []
            for h in (h0, h1):
                seg = jnp.where(tri, cum[:, h:h + 1] - cum_t[h:h + 1, :], NEG_INF)
                lmats.append((cb * jnp.exp(seg) * dt_t[h:h + 1, :]).astype(BF16))
            x_lo = jnp.where(lane_lo, x_pair, 0.0).astype(BF16)
            x_hi = jnp.where(lane_lo, 0.0, x_pair).astype(BF16)
            yd = (jnp.dot(lmats[0], x_lo, preferred_element_type=F32)
                  + jnp.dot(lmats[1], x_hi, preferred_element_type=F32))
            scal = jnp.where(lane_lo, ecum[:, h0:h0 + 1], ecum[:, h1:h1 + 1])
            yo = yo_g[:, 2 * pr * SSD_HEADDIM: 2 * pr * SSD_HEADDIM + LANES]
            y = yd + yo * scal + dexp_ref[:, c0:c0 + LANES] * x_pair
            zz = z[:, c0:c0 + LANES]
            gated_ref[:, c0:c0 + LANES] = y * _silu(zz)
            wsel = jnp.where(lane_lo, wend[:, h0:h0 + 1], wend[:, h1:h1 + 1])
            xw_t = (x_pair * wsel).T.astype(BF16)
            cs = jnp.dot(xw_t, b_bf, preferred_element_type=F32)
            dec = jnp.concatenate(
                [jnp.broadcast_to(dec_h[h0:h0 + 1, :], (SSD_HEADDIM, LANES)),
                 jnp.broadcast_to(dec_h[h1:h1 + 1, :], (SSD_HEADDIM, LANES))], axis=0)
            r_lo = h0 * SSD_HEADDIM
            s_ref[r_lo:r_lo + LANES, :] = dec * s_ref[r_lo:r_lo + LANES, :] + cs

    outs = []
    for g in range(SSD_GROUPS):
        gg = gated_ref[:, g * 512:(g + 1) * 512]
        ms = jnp.mean(gg * gg, axis=-1, keepdims=True)
        outs.append(((gg * lax.rsqrt(ms + NORM_EPS)).astype(F32)
                     * nrm_ref[:, g * 512:(g + 1) * 512]).astype(BF16))
    yn = jnp.concatenate(outs, axis=1)
    return ya, yn


def _mixer_prompt_kernel(main_ref, dt_ref, caw_ref, scw_ref, scb_ref, dtb_ref, alog_ref,
                         dexp_ref, nrm_ref,
                         ya_ref, yn_ref, nca_ref, ncs_ref, nss_ref,
                         su_ref, sx_ref, s_ref, gated_ref):
    j = pl.program_id(1)

    @pl.when(j == 0)
    def _():
        su_ref[0:8, :] = jnp.zeros((8, D_MODEL), F32)
        sx_ref[0:8, :] = jnp.zeros((8, SSD_CONV_DIM), F32)
        s_ref[...] = jnp.zeros_like(s_ref)

    a_b = main_ref[:, COL_AB:COL_AB + 1024]
    u = main_ref[:, COL_AC:COL_AC + 1024] * main_ref[:, COL_AX:COL_AX + 1024]
    z = main_ref[:, COL_Z:COL_Z + 2048]
    xbc = main_ref[:, COL_XBC:COL_XBC + SSD_CONV_DIM]
    r0 = jnp.where(j == 0, ROWS - N_META, 0)
    ya, yn = _mixer_block(a_b, u, z, xbc, dt_ref[...], r0, su_ref, sx_ref, s_ref, gated_ref,
                          caw_ref, scw_ref, scb_ref, dtb_ref, alog_ref, dexp_ref, nrm_ref)
    ya_ref[...] = ya
    yn_ref[...] = yn
    nca_ref[0] = su_ref[6:8, :]
    ncs_ref[0] = sx_ref[5:8, :]
    nss_ref[0] = s_ref[...]


def _mixer_sample_kernel(main_ref, dt_ref, ca0_ref, cs0_ref, ss0_ref,
                         caw_ref, scw_ref, scb_ref, dtb_ref, alog_ref, dexp_ref, nrm_ref,
                         ya_ref, yn_ref, nca_ref, ncs_ref, nss_ref,
                         su_ref, sx_ref, s_ref, gated_ref, pad_ref, tmp_ref):
    R = ROWS
    n = 8
    su_ref[0:8, :] = jnp.zeros((8, D_MODEL), F32)
    sx_ref[0:8, :] = jnp.zeros((8, SSD_CONV_DIM), F32)
    s_ref[...] = ss0_ref[0]

    pad_ref[...] = jnp.zeros_like(pad_ref)
    pad_ref[R - n:R, :] = main_ref[...]
    a_b = pad_ref[:, COL_AB:COL_AB + 1024]
    z = pad_ref[:, COL_Z:COL_Z + 2048]
    tmp_ref[...] = jnp.zeros_like(tmp_ref)
    tmp_ref[6:8, 0:D_MODEL] = ca0_ref[0]
    u = jnp.concatenate(
        [jnp.zeros((R - 2 * n, D_MODEL), F32), tmp_ref[:, 0:D_MODEL],
         main_ref[:, COL_AC:COL_AC + 1024] * main_ref[:, COL_AX:COL_AX + 1024]], axis=0)
    tmp_ref[...] = jnp.zeros_like(tmp_ref)
    tmp_ref[5:8, :] = cs0_ref[0]
    xbc = jnp.concatenate(
        [jnp.zeros((R - 2 * n, SSD_CONV_DIM), F32), tmp_ref[...],
         main_ref[:, COL_XBC:COL_XBC + SSD_CONV_DIM]], axis=0)
    dt_raw = jnp.concatenate([jnp.zeros((R - n, LANES), F32), dt_ref[...]], axis=0)
    ya, yn = _mixer_block(a_b, u, z, xbc, dt_raw, R - n, su_ref, sx_ref, s_ref, gated_ref,
                          caw_ref, scw_ref, scb_ref, dtb_ref, alog_ref, dexp_ref, nrm_ref)
    ya_ref[...] = ya[R - n:R, :]
    yn_ref[...] = yn[R - n:R, :]
    nca_ref[0] = su_ref[6:8, :]
    ncs_ref[0] = sx_ref[5:8, :]
    nss_ref[0] = s_ref[...]


def _mixer_weight_specs(nidx):
    zero = (lambda *a: (0, 0))
    del nidx
    return [
        pl.BlockSpec((CONV_A_W, D_MODEL), zero),
        pl.BlockSpec((SSD_CONV_W, SSD_CONV_DIM), zero),
        pl.BlockSpec((1, SSD_CONV_DIM), zero),
        pl.BlockSpec((1, LANES), zero),
        pl.BlockSpec((1, LANES), zero),
        pl.BlockSpec((1, SSD_INNER), zero),
        pl.BlockSpec((1, SSD_INNER), zero),
    ]


def _mixer_scratch():
    return [
        pltpu.VMEM((ROWS + 8, D_MODEL), F32),
        pltpu.VMEM((ROWS + 8, SSD_CONV_DIM), F32),
        pltpu.VMEM((SSD_INNER, SSD_STATE), F32),
        pltpu.VMEM((ROWS, SSD_INNER), F32),
    ]


def _mixer_prompt(proj, weights, nseq, nblk, meta_row0):
    dt_blk = COL_DT // LANES
    meta_blk0 = meta_row0 // ROWS
    rblk = lambda b, j: jnp.where(j == 0, meta_blk0 + b, b * (nblk - 1) + j - 1)
    return pl.pallas_call(
        _mixer_prompt_kernel,
        grid=(nseq, nblk),
        in_specs=[
            pl.BlockSpec((ROWS, MAIN_W), lambda b, j: (rblk(b, j), 0)),
            pl.BlockSpec((ROWS, LANES), lambda b, j: (rblk(b, j), dt_blk)),
        ] + _mixer_weight_specs(2),
        out_specs=[
            pl.BlockSpec((ROWS, D_MODEL), lambda b, j: (rblk(b, j), 0)),
            pl.BlockSpec((ROWS, SSD_INNER), lambda b, j: (rblk(b, j), 0)),
            pl.BlockSpec((1, CONV_A_W - 1, D_MODEL), lambda b, j: (b, 0, 0)),
            pl.BlockSpec((1, SSD_CONV_W - 1, SSD_CONV_DIM), lambda b, j: (b, 0, 0)),
            pl.BlockSpec((1, SSD_INNER, SSD_STATE), lambda b, j: (b, 0, 0)),
        ],
        out_shape=[
            jax.ShapeDtypeStruct((proj.shape[0], D_MODEL), BF16),
            jax.ShapeDtypeStruct((proj.shape[0], SSD_INNER), BF16),
            jax.ShapeDtypeStruct((nseq, CONV_A_W - 1, D_MODEL), F32),
            jax.ShapeDtypeStruct((nseq, SSD_CONV_W - 1, SSD_CONV_DIM), F32),
            jax.ShapeDtypeStruct((nseq, SSD_INNER, SSD_STATE), F32),
        ],
        scratch_shapes=_mixer_scratch(),
        compiler_params=_cparams(("arbitrary", "arbitrary")),
        name="mixer_prompt",
    )(proj, proj, *weights)


def _mixer_sample(proj, ca0, cs0, ss0, weights, nseq, row0):
    dt_blk = COL_DT // LANES
    blk0 = row0 // 8
    return pl.pallas_call(
        _mixer_sample_kernel,
        grid=(nseq,),
        in_specs=[
            pl.BlockSpec((8, MAIN_W), lambda s: (blk0 + s, 0)),
            pl.BlockSpec((8, LANES), lambda s: (blk0 + s, dt_blk)),
            pl.BlockSpec((1, CONV_A_W - 1, D_MODEL), lambda s: (s, 0, 0)),
            pl.BlockSpec((1, SSD_CONV_W - 1, SSD_CONV_DIM), lambda s: (s, 0, 0)),
            pl.BlockSpec((1, SSD_INNER, SSD_STATE), lambda s: (s, 0, 0)),
        ] + _mixer_weight_specs(1),
        out_specs=[
            pl.BlockSpec((8, D_MODEL), lambda s: (s, 0)),
            pl.BlockSpec((8, SSD_INNER), lambda s: (s, 0)),
            pl.BlockSpec((1, CONV_A_W - 1, D_MODEL), lambda s: (s, 0, 0)),
            pl.BlockSpec((1, SSD_CONV_W - 1, SSD_CONV_DIM), lambda s: (s, 0, 0)),
            pl.BlockSpec((1, SSD_INNER, SSD_STATE), lambda s: (s, 0, 0)),
        ],
        out_shape=[
            jax.ShapeDtypeStruct((nseq * 8, D_MODEL), BF16),
            jax.ShapeDtypeStruct((nseq * 8, SSD_INNER), BF16),
            jax.ShapeDtypeStruct((nseq, CONV_A_W - 1, D_MODEL), F32),
            jax.ShapeDtypeStruct((nseq, SSD_CONV_W - 1, SSD_CONV_DIM), F32),
            jax.ShapeDtypeStruct((nseq, SSD_INNER, SSD_STATE), F32),
        ],
        scratch_shapes=_mixer_scratch() + [
            pltpu.VMEM((ROWS, MAIN_W), F32),
            pltpu.VMEM((8, SSD_CONV_DIM), F32),
        ],
        compiler_params=_cparams(("arbitrary",)),
        name="mixer_sample",
    )(proj, proj, ca0, cs0, ss0, *weights)


def _merge_kernel(s_lo, s_hi, x_ref, yap_ref, yas_ref, ynp_ref, yns_ref, g_ref, wa_ref, ws_ref,
                  wo_ref, nw_ref, x2_ref, hn_ref):
    i = pl.program_id(0)
    is_sample = jnp.logical_and(i >= s_lo, i < s_hi)
    ya = jnp.where(is_sample, yas_ref[...], yap_ref[...])
    yn = jnp.where(is_sample, yns_ref[...], ynp_ref[...])
    br_a = jnp.dot(ya, wa_ref[...], preferred_element_type=F32)
    br_b = jnp.dot(yn, ws_ref[...], preferred_element_type=F32)
    g = g_ref[...]
    merged = jax.nn.sigmoid(g[:, :D_MODEL]) * br_a + jax.nn.sigmoid(g[:, D_MODEL:]) * br_b
    mix = jnp.dot(merged.astype(BF16), wo_ref[...], preferred_element_type=F32)
    x2 = x_ref[...] + mix
    x2_ref[...] = x2
    inv = lax.rsqrt(jnp.mean(x2 * x2, axis=-1, keepdims=True) + NORM_EPS)
    hn_ref[...] = ((x2 * inv) * nw_ref[...]).T.astype(BF16)


def _merge(x_all, ya_p, ya_s, yn_p, yn_s, proj, wa, ws, wo, nw, tt, sample_row0):
    t = x_all.shape[0]
    assert sample_row0 % tt == 0 and ya_s.shape[0] % tt == 0 and ya_p.shape[0] == t
    s_lo = sample_row0 // tt
    s_hi = s_lo + ya_s.shape[0] // tt
    in_s = lambda i: jnp.logical_and(i >= s_lo, i < s_hi)
    p_map = lambda i: (jnp.where(in_s(i), 0, i), 0)
    s_map = lambda i: (jnp.clip(i - s_lo, 0, s_hi - s_lo - 1), 0)
    gblk = COL_G // (2 * D_MODEL)
    zero = lambda i: (0, 0)
    return pl.pallas_call(
        functools.partial(_merge_kernel, s_lo, s_hi),
        grid=(t // tt,),
        in_specs=[
            pl.BlockSpec((tt, D_MODEL), lambda i: (i, 0)),
            pl.BlockSpec((tt, D_MODEL), p_map),
            pl.BlockSpec((tt, D_MODEL), s_map),
            pl.BlockSpec((tt, SSD_INNER), p_map),
            pl.BlockSpec((tt, SSD_INNER), s_map),
            pl.BlockSpec((tt, 2 * D_MODEL), lambda i: (i, gblk)),
            pl.BlockSpec((D_MODEL, D_MODEL), zero),
            pl.BlockSpec((SSD_INNER, D_MODEL), zero),
            pl.BlockSpec((D_MODEL, D_MODEL), zero),
            pl.BlockSpec((1, D_MODEL), zero),
        ],
        out_specs=[
            pl.BlockSpec((tt, D_MODEL), lambda i: (i, 0)),
            pl.BlockSpec((D_MODEL, tt), lambda i: (0, i)),
        ],
        out_shape=[
            jax.ShapeDtypeStruct((t, D_MODEL), F32),
            jax.ShapeDtypeStruct((D_MODEL, t), BF16),
        ],
        compiler_params=_cparams(("arbitrary",)),
        name="merge",
    )(x_all, ya_p, ya_s, yn_p, yn_s, proj, wa, ws, wo, nw)


def _oddeven_merge_sort_pairs(n):
    pairs = []
    p = 1
    while p < n:
        k = p
        while k >= 1:
            j = k % p
            while j <= n - 1 - k:
                for i in range(min(k, n - j - k)):
                    if (i + j) // (2 * p) == (i + j + k) // (2 * p):
                        pairs.append((i + j, i + j + k))
                j += 2 * k
            k //= 2
        p *= 2
    return pairs


_SORT16 = _oddeven_merge_sort_pairs(16)
_BITONIC16 = [(i, i + d) for d in (8, 4, 2, 1) for i in range(16) if (i & d) == 0]
_CAND_PAIRS = [(r1, r2) for r1 in range(PEER_TOPK) for r2 in range(PEER_TOPK)
               if (r1 + 1) * (r2 + 1) <= PEER_TOPK]


def _ce(v, i, j):
    hi = jnp.maximum(v[i], v[j])
    lo = jnp.minimum(v[i], v[j])
    v[i], v[j] = hi, lo


def _top16_desc(v):
    v = list(v)
    for i, j in _SORT16:
        _ce(v, i, j)
    for shift in (4, 2, 1):
        p = [pltpu.roll(x, shift, 0) for x in v]
        v = [jnp.maximum(v[r], p[15 - r]) for r in range(16)]
        for i, j in _BITONIC16:
            _ce(v, i, j)
    return v


def _route_kernel(ht_ref, wq_ref, keys_ref, s1_ref, e1_ref, s2_ref, e2_ref, tau_ref, zinv_ref, q_ref):
    tt = ht_ref.shape[1]
    q_ref[...] = jnp.dot(wq_ref[...], ht_ref[...], preferred_element_type=F32).astype(BF16)
    sub = lax.broadcasted_iota(jnp.int32, (SUBLANES, tt), 0)
    tops = [[None] * PEER_TOPK, [None] * PEER_TOPK]
    s_refs, e_refs = (s1_ref, s2_ref), (e1_ref, e2_ref)
    for h in range(PEER_HEADS):
        for i in range(2):
            r0 = (h * 2 + i) * PEER_HALF
            s = jnp.dot(keys_ref[h * 2 + i], q_ref[r0:r0 + PEER_HALF, :],
                        preferred_element_type=F32)
            v = [s[8 * j:8 * j + 8, :] for j in range(16)]
            top = _top16_desc(v)
            thr, best = top[PEER_TOPK - 1], top[0]
            for j in range(16):
                s_refs[i][h, j] = jnp.where(v[j] >= thr, v[j], NEG_INF)
                e_refs[i][h, j] = jnp.exp(v[j] - best)
            for r in range(PEER_TOPK):
                tops[i][r] = top[r] if h == 0 else jnp.where(sub == h, top[r], tops[i][r])
    cand = [tops[0][r1] + tops[1][r2] for r1, r2 in _CAND_PAIRS]
    picked = []
    for _ in range(PEER_TOPK):
        m = cand[0]
        for c in cand[1:]:
            m = jnp.maximum(m, c)
        picked.append(m)
        found = jnp.zeros(m.shape, jnp.bool_)
        nxt = []
        for c in cand:
            eq = c == m
            take = jnp.logical_and(eq, jnp.logical_not(found))
            found = jnp.logical_or(found, eq)
            nxt.append(jnp.where(take, NEG_INF, c))
        cand = nxt
    zsum = jnp.zeros_like(picked[0])
    for c in picked:
        zsum = zsum + jnp.exp(c - picked[0])
    tau_ref[...] = picked[PEER_TOPK - 1]
    zinv_ref[...] = 1.0 / zsum


def _route(hn_t, wq_t, keys, tt):
    t = hn_t.shape[1]
    tab_shape = (PEER_HEADS, N_KEYS // SUBLANES, SUBLANES, t)
    tab_spec = pl.BlockSpec((PEER_HEADS, N_KEYS // SUBLANES, SUBLANES, tt), lambda i: (0, 0, 0, i))
    return pl.pallas_call(
        _route_kernel,
        grid=(t // tt,),
        in_specs=[
            pl.BlockSpec((D_MODEL, tt), lambda i: (0, i)),
            pl.BlockSpec(wq_t.shape, lambda i: (0, 0)),
            pl.BlockSpec(keys.shape, lambda i: (0, 0, 0)),
        ],
        out_specs=[tab_spec] * 4 + [
            pl.BlockSpec((PEER_HEADS, tt), lambda i: (0, i)),
            pl.BlockSpec((PEER_HEADS, tt), lambda i: (0, i)),
        ],
        out_shape=[jax.ShapeDtypeStruct(tab_shape, F32)] * 4 + [
            jax.ShapeDtypeStruct((PEER_HEADS, t), F32),
            jax.ShapeDtypeStruct((PEER_HEADS, t), F32),
        ],
        scratch_shapes=[pltpu.VMEM((wq_t.shape[0], tt), BF16)],
        compiler_params=_cparams(("arbitrary",)),
        name="peer_route",
    )(hn_t, wq_t, keys)


I1_PER_STEP = 32
I1_GROUPS = I1_PER_STEP // SUBLANES
I1_GROUP = 4
EXPERTS_PER_STEP = I1_PER_STEP * N_KEYS


def _expert_kernel(n_first, ht_ref, u_ref, vt_ref, s1_ref, e1_ref, s2_ref, e2_ref, tau_ref,
                   zinv_ref, x2_ref, nfw_ref, ya_ref, yb_ref, acc_ref, w_ref, act_ref, e1z_ref):
    c = pl.program_id(1)
    tt = ht_ref.shape[1]

    @pl.when(c == 0)
    def _():
        acc_ref[...] = jnp.zeros_like(acc_ref)

    for h in range(PEER_HEADS):
        for gi in range(I1_GROUPS):
            e1z_ref[h, gi * SUBLANES:(gi + 1) * SUBLANES, :] = (
                e1_ref[h, gi] * (0.5 * zinv_ref[h:h + 1, :]))

    for j in range(I1_PER_STEP):
        act_ref[j % 2] = jnp.dot(u_ref[j * N_KEYS:(j + 1) * N_KEYS, :], ht_ref[...],
                                 preferred_element_type=F32)
        for half in range(tt // LANES):
            ls = pl.ds(half * LANES, LANES)
            taub = [jnp.broadcast_to(tau_ref[h:h + 1, ls], (BF16_ROWS, LANES))
                    for h in range(PEER_HEADS)]
            s1b = [jnp.broadcast_to(s1_ref[h, j // SUBLANES, j % SUBLANES:j % SUBLANES + 1, ls],
                                    (BF16_ROWS, LANES))
                   for h in range(PEER_HEADS)]
            e1b = [jnp.broadcast_to(e1z_ref[h, j:j + 1, ls], (BF16_ROWS, LANES))
                   for h in range(PEER_HEADS)]
            for bb in range(N_KEYS // BF16_ROWS):
                r = bb * BF16_ROWS
                g = None
                for h in range(PEER_HEADS):
                    v = s1b[h] + s2_ref[h, r:r + BF16_ROWS, ls]
                    t = jnp.where(v >= taub[h], e1b[h] * e2_ref[h, r:r + BF16_ROWS, ls], 0.0)
                    g = t if g is None else g + t
                a = act_ref[j % 2, r:r + BF16_ROWS, ls]
                ge = a * (1.0 + lax.erf(a * math.sqrt(0.5)))
                w_ref[j * N_KEYS + r:j * N_KEYS + r + BF16_ROWS, ls] = (ge * g).astype(BF16)
        if j % I1_GROUP == I1_GROUP - 1:
            k0 = (j + 1 - I1_GROUP) * N_KEYS
            acc_ref[...] += jnp.dot(vt_ref[:, k0:k0 + I1_GROUP * N_KEYS],
                                    w_ref[k0:k0 + I1_GROUP * N_KEYS, :], preferred_element_type=F32)

    @pl.when(c == pl.num_programs(1) - 1)
    def _():
        x3 = x2_ref[...] + acc_ref[...].T
        inv = lax.rsqrt(jnp.mean(x3 * x3, axis=-1, keepdims=True) + NORM_EPS)
        y = (x3 * inv) * nfw_ref[...]
        first = pl.program_id(0) < n_first

        @pl.when(first)
        def _():
            ya_ref[...] = y

        @pl.when(jnp.logical_not(first))
        def _():
            yb_ref[...] = y


def _experts(hn_t, u_bf, v_t, s1, e1, s2, e2, tau, zinv, x2, nfw, tt, rows_a):
    t = hn_t.shape[1]
    n_exp = u_bf.shape[0]
    nc = n_exp // EXPERTS_PER_STEP
    assert rows_a % tt == 0 and 0 < rows_a < t
    n_first = rows_a // tt
    assert s1.shape == (PEER_HEADS, nc * I1_GROUPS, SUBLANES, t)
    s2 = s2.reshape(PEER_HEADS, N_KEYS, t)
    e2 = e2.reshape(PEER_HEADS, N_KEYS, t)
    return pl.pallas_call(
        functools.partial(_expert_kernel, n_first),
        grid=(t // tt, nc),
        in_specs=[
            pl.BlockSpec((D_MODEL, tt), lambda i, c: (0, i)),
            pl.BlockSpec((EXPERTS_PER_STEP, D_MODEL), lambda i, c: (c, 0)),
            pl.BlockSpec((D_MODEL, EXPERTS_PER_STEP), lambda i, c: (0, c)),
            pl.BlockSpec((PEER_HEADS, I1_GROUPS, SUBLANES, tt), lambda i, c: (0, c, 0, i)),
            pl.BlockSpec((PEER_HEADS, I1_GROUPS, SUBLANES, tt), lambda i, c: (0, c, 0, i)),
            pl.BlockSpec((PEER_HEADS, N_KEYS, tt), lambda i, c: (0, 0, i)),
            pl.BlockSpec((PEER_HEADS, N_KEYS, tt), lambda i, c: (0, 0, i)),
            pl.BlockSpec((PEER_HEADS, tt), lambda i, c: (0, i)),
            pl.BlockSpec((PEER_HEADS, tt), lambda i, c: (0, i)),
            pl.BlockSpec((tt, D_MODEL), lambda i, c: (i, 0)),
            pl.BlockSpec((1, D_MODEL), lambda i, c: (0, 0)),
        ],
        out_specs=[
            pl.BlockSpec((tt, D_MODEL), lambda i, c: (jnp.minimum(i, n_first - 1), 0)),
            pl.BlockSpec((tt, D_MODEL), lambda i, c: (jnp.maximum(i - n_first, 0), 0)),
        ],
        out_shape=[
            jax.ShapeDtypeStruct((rows_a, D_MODEL), F32),
            jax.ShapeDtypeStruct((t - rows_a, D_MODEL), F32),
        ],
        scratch_shapes=[
            pltpu.VMEM((D_MODEL, tt), F32),
            pltpu.VMEM((EXPERTS_PER_STEP, tt), BF16),
            pltpu.VMEM((2, N_KEYS, tt), F32),
            pltpu.VMEM((PEER_HEADS, I1_PER_STEP, tt), F32),
        ],
        compiler_params=_cparams(("arbitrary", "arbitrary")),
        name="peer_experts",
    )(hn_t, u_bf, v_t, s1, e1, s2, e2, tau, zinv, x2, nfw)


def _pick_tile(t, prefs):
    for p in prefs:
        if t % p == 0:
            return p
    raise ValueError(f"token count {t} not tileable by {prefs}")


def _layer(x_all, nseq_p, nblk, ca0, cs0, ss0, lw, norm_final_w):
    (norm_mix_w, w_in, conv_a_w, w_a_out, ssd_conv_w, ssd_conv_b, ssd_dt_bias, ssd_a_log, ssd_d,
     ssd_norm_w, w_ssd_out, w_o, norm_ffn_w, peer_w_q, peer_sub_keys, peer_u, peer_v) = lw
    t_all = x_all.shape[0]
    t_real = nseq_p * (nblk - 1) * ROWS
    nseq_s = ca0.shape[0]
    t_s = nseq_s * 8

    c = np.cumsum([0, 1024, 1024, 1024, SSD_INNER, SSD_CONV_DIM, SSD_HEADS, 1024, 1024])
    w_in_r = jnp.concatenate(
        [w_in[:, c[0]:c[5]], w_in[:, c[6]:c[8]], w_in[:, c[5]:c[6]],
         jnp.zeros((D_MODEL, IN_DIM_R - COL_DT - SSD_HEADS), w_in.dtype)], axis=1).astype(BF16)
    pad_h = lambda v: jnp.pad(v.astype(F32), (0, LANES - SSD_HEADS)).reshape(1, LANES)
    mixer_w = (conv_a_w.astype(F32), ssd_conv_w.astype(F32), ssd_conv_b.reshape(1, -1).astype(F32),
               pad_h(ssd_dt_bias), pad_h(ssd_a_log),
               jnp.repeat(ssd_d.astype(F32), SSD_HEADDIM).reshape(1, SSD_INNER),
               ssd_norm_w.reshape(1, SSD_INNER).astype(F32))

    proj = _inproj(x_all, norm_mix_w.reshape(1, D_MODEL), w_in_r,
                   _pick_tile(t_all, (1024, 512, 256)), INPROJ_NT)
    ya_p, yn_p, nca_p, ncs_p, nss_p = _mixer_prompt(proj, mixer_w, nseq_p, nblk, t_real + t_s)
    ya_s, yn_s, nca_s, ncs_s, nss_s = _mixer_sample(
        proj, ca0, cs0, ss0.reshape(nseq_s, SSD_INNER, SSD_STATE), mixer_w, nseq_s, t_real)

    x2, hn_t = _merge(x_all, ya_p, ya_s, yn_p, yn_s, proj, w_a_out.astype(BF16), w_ssd_out.astype(BF16),
                      w_o.astype(BF16), norm_ffn_w.reshape(1, D_MODEL),
                      _pick_tile(math.gcd(t_real, t_s), (512, 256)), t_real)

    wq_t = peer_w_q.T.astype(BF16)
    keys = peer_sub_keys.reshape(PEER_HEADS * 2, N_KEYS, PEER_HALF).astype(BF16)
    s1, e1, s2, e2, tau, zinv = _route(hn_t, wq_t, keys, _pick_tile(t_all, (256, 128)))
    y_real, y_rest = _experts(hn_t, peer_u.astype(BF16), peer_v.T.astype(BF16), s1, e1, s2, e2, tau,
                              zinv, x2, norm_final_w.reshape(1, D_MODEL),
                              _pick_tile(t_all, (256, 128)), t_real)
    shp = (SSD_HEADS, SSD_HEADDIM, SSD_STATE)
    return (y_real, y_rest[:t_s], nca_p, ncs_p, nss_p.reshape(nseq_p, *shp),
            nca_s, ncs_s, nss_s.reshape(nseq_s, *shp))


def kernel(x_prompt, x_sample, state_conv_a, state_conv_ssd, state_ssm, meta_tokens, norm_mix_w, w_in, conv_a_w, w_a_out, ssd_conv_w, ssd_conv_b, ssd_dt_bias, ssd_a_log, ssd_d, ssd_norm_w, w_ssd_out, w_o, norm_ffn_w, peer_w_q, peer_sub_keys, peer_u, peer_v, norm_final_w):
    bp, s_len, d = x_prompt.shape
    bs, ds_len, _ = x_sample.shape
    depth = w_in.shape[0]
    assert depth == 1 and d == D_MODEL and ds_len == 8 and s_len % ROWS == 0
    nblk = s_len // ROWS + 1
    head = jnp.concatenate(
        [jnp.zeros((ROWS - N_META, d), x_prompt.dtype), meta_tokens.astype(x_prompt.dtype)], axis=0)
    x_all = jnp.concatenate(
        [x_prompt.reshape(-1, d), x_sample.reshape(-1, d),
         jnp.broadcast_to(head[None], (bp, ROWS, d)).reshape(-1, d)], axis=0)
    sq = lambda a: a.reshape(a.shape[1:])
    lw = tuple(sq(a) for a in (
        norm_mix_w, w_in, conv_a_w, w_a_out, ssd_conv_w, ssd_conv_b, ssd_dt_bias, ssd_a_log,
        ssd_d, ssd_norm_w, w_ssd_out, w_o, norm_ffn_w, peer_w_q, peer_sub_keys, peer_u, peer_v))
    y_real, y_s, nca_p, ncs_p, nss_p, nca_s, ncs_s, nss_s = _layer(
        x_all, bp, nblk, sq(state_conv_a), sq(state_conv_ssd), sq(state_ssm), lw, norm_final_w)
    y_prompt = y_real.reshape(bp, s_len, d)
    y_sample = y_s.reshape(bs, ds_len, d)
    return (y_prompt, y_sample, nca_p[None], ncs_p[None], nss_p[None],
            nca_s[None], ncs_s[None], nss_s[None])
```
